```python
import math
import jax
import jax.numpy as jnp
from jax import lax
import numpy as np

D_MODEL = 1024
BATCH = 16
SEQ = 2048
DEPTH = 2

HEAD_DIM = 64
DIFF_HEADS = 4
DIFF_V_DIM = 2 * HEAD_DIM
NSA_HEADS = 8
NSA_KV_GROUPS = 2
CMP_LEN = 32
CMP_STRIDE = 16
CMP_HIDDEN = 256
SLC_BLOCK = 64
SLC_TOPK = 16
NSA_WINDOW = 512
FORCED_SCORE = 1e4
SWA_HEADS = 8
SWA_KV_HEADS = 2
SWA_WINDOW = 128
N_BRANCHES = 3
BRANCH_WIDTH = SWA_HEADS * HEAD_DIM
D_FF = ((8 * D_MODEL // 3 + 255) // 256) * 256
PLE_DIM = 256
QUERY_BLOCK = 128
SLC_QUERY_CHUNK = 32
LN_EPS = 1e-5
SUBLN_EPS = 1e-5
ALPHA = (2 * DEPTH) ** 0.25
BETA = (8 * DEPTH) ** -0.25
NEG_INF = -1e30

IN_SPLITS = (
    DIFF_HEADS * 2 * HEAD_DIM,
    DIFF_HEADS * 2 * HEAD_DIM,
    DIFF_HEADS * DIFF_V_DIM,
    NSA_HEADS * HEAD_DIM,
    2 * NSA_KV_GROUPS * HEAD_DIM,
    2 * NSA_KV_GROUPS * HEAD_DIM,
    2 * NSA_KV_GROUPS * HEAD_DIM,
    3 * NSA_HEADS,
    SWA_HEADS * HEAD_DIM,
    SWA_KV_HEADS * HEAD_DIM,
    SWA_KV_HEADS * HEAD_DIM,
    N_BRANCHES * D_MODEL,
)
D_IN = sum(IN_SPLITS)

kernel_name = 'hybrid_diff_nsa_swa_macaron_deepnorm'


def _split_offsets():
    offs, acc = [], 0
    for n in IN_SPLITS[:-1]:
        acc += n
        offs.append(acc)
    return offs


def layer_norm(x, g, b):
    xf = x.astype(jnp.float32)
    mu = jnp.mean(xf, axis=-1, keepdims=True)
    var = jnp.mean(jnp.square(xf - mu), axis=-1, keepdims=True)
    return ((xf - mu) * lax.rsqrt(var + LN_EPS) * g + b).astype(x.dtype)


def swiglu(x, w_in, w_out):
    gate, up = jnp.split(x @ w_in, 2, axis=-1)
    return (jax.nn.silu(gate) * up) @ w_out


def alibi_slopes(n_heads):
    return jnp.exp2(-8.0 * jnp.arange(1, n_heads + 1, dtype=jnp.float32) / n_heads)


def diff_attention(q, k, v, lam, subln_g, lambda_init):
    B, H, S, _, d = q.shape
    scale = d ** -0.5
    lam = lam.astype(jnp.float32)
    lam_full = jnp.exp(jnp.dot(lam[0], lam[1])) - jnp.exp(jnp.dot(lam[2], lam[3])) + lambda_init
    slopes = alibi_slopes(H)[None, :, None, None, None]
    kpos = jnp.arange(S)

    def block(i):
        start = i * QUERY_BLOCK
        q_blk = lax.dynamic_slice_in_dim(q, start, QUERY_BLOCK, axis=2)
        dist = (start + jnp.arange(QUERY_BLOCK))[:, None] - kpos[None, :]
        s = jnp.einsum('bhqmd,bhkmd->bhmqk', q_blk, k, preferred_element_type=jnp.float32) * scale
        s = jnp.where(dist >= 0, s - slopes * dist, NEG_INF)
        a = jax.nn.softmax(s, axis=-1)
        w = a[:, :, 0] - lam_full * a[:, :, 1]
        return jnp.einsum('bhqk,bhkv->bhqv', w.astype(v.dtype), v)

    o = lax.map(block, jnp.arange(S // QUERY_BLOCK))
    o = jnp.moveaxis(o, 0, 2).reshape(B, H, S, -1).astype(jnp.float32)
    o = o * lax.rsqrt(jnp.mean(o * o, axis=-1, keepdims=True) + SUBLN_EPS) * subln_g
    return (o * (1.0 - lambda_init)).astype(v.dtype)


def banded_attention(q, k, v, slopes, window, sinks=None):
    B, G, R, S, d = q.shape
    scale = d ** -0.5
    span = QUERY_BLOCK + window
    kp = jnp.pad(k, ((0, 0), (0, 0), (window, 0), (0, 0)))
    vp = jnp.pad(v, ((0, 0), (0, 0), (window, 0), (0, 0)))
    sl = slopes.reshape(G, R)[None, :, :, None, None]

    def block(i):
        start = i * QUERY_BLOCK
        q_blk = lax.dynamic_slice_in_dim(q, start, QUERY_BLOCK, axis=3)
        k_blk = lax.dynamic_slice_in_dim(kp, start, span, axis=2)
        v_blk = lax.dynamic_slice_in_dim(vp, start, span, axis=2)
        qpos = start + jnp.arange(QUERY_BLOCK)
        kpos = start - window + jnp.arange(span)
        dist = qpos[:, None] - kpos[None, :]
        valid = (dist >= 0) & (dist < window) & (kpos[None, :] >= 0)
        s = jnp.einsum('bgrqd,bgkd->bgrqk', q_blk, k_blk, preferred_element_type=jnp.float32) * scale
        s = jnp.where(valid, s - sl * dist, NEG_INF)
        if sinks is None:
            a = jax.nn.softmax(s, axis=-1)
        else:
            sink = jnp.broadcast_to(sinks.astype(jnp.float32).reshape(G, R)[None, :, :, None, None],
                                    s.shape[:-1] + (1,))
            a = jax.nn.softmax(jnp.concatenate([s, sink], axis=-1), axis=-1)[..., :-1]
        return jnp.einsum('bgrqk,bgkd->bgrqd', a.astype(v_blk.dtype), v_blk)

    o = lax.map(block, jnp.arange(S // QUERY_BLOCK))
    return jnp.moveaxis(o, 0, 4).reshape(B, G, R, S, d)


def nsa_attention(q, kv_cmp, kv_slc, kv_win, gates, cmp_pos, cmp_w1, cmp_w2):
    B, G, R, S, d = q.shape
    scale = d ** -0.5
    slopes = alibi_slopes(G * R)
    sl = slopes.reshape(G, R)[None, :, :, None, None]
    tpos = jnp.arange(S)

    n_cmp = (S - CMP_LEN) // CMP_STRIDE + 1
    c_start = jnp.arange(n_cmp) * CMP_STRIDE
    idx = c_start[:, None] + jnp.arange(CMP_LEN)[None, :]
    blocks = kv_cmp[:, :, :, idx] + cmp_pos[:, None, None, None]
    hid = jax.nn.gelu(jnp.einsum('nbgcf,nfh->nbgch',
                                 blocks.reshape(2, B, G, n_cmp, CMP_LEN * d), cmp_w1))
    k_c, v_c = jnp.einsum('nbgch,nhd->nbgcd', hid, cmp_w2)
    dist_c = tpos[:, None] - (c_start + CMP_LEN - 1)[None, :]
    valid_c = dist_c >= 0
    s_c = jnp.einsum('bgrtd,bgcd->bgrtc', q, k_c, preferred_element_type=jnp.float32) * scale
    s_c = jnp.where(valid_c, s_c - sl * dist_c, NEG_INF)
    p_c = jax.nn.softmax(s_c, axis=-1) * jnp.any(valid_c, axis=-1)[:, None]
    o_cmp = jnp.einsum('bgrtc,bgcd->bgrtd', p_c.astype(v_c.dtype), v_c)

    n_blk = S // SLC_BLOCK
    n_sel = min(SLC_TOPK, n_blk)
    j_start = jnp.arange(n_blk) * SLC_BLOCK
    overlap = (jnp.minimum(c_start[:, None] + CMP_LEN, j_start[None, :] + SLC_BLOCK)
               - jnp.maximum(c_start[:, None], j_start[None, :]))
    cmp_to_slc = jnp.maximum(overlap, 0).astype(jnp.float32) / CMP_LEN
    score = jnp.einsum('bgrtc,cj->bgtj', p_c, cmp_to_slc)
    t_blk = (tpos // SLC_BLOCK)[:, None]
    jj = jnp.arange(n_blk)[None, :]
    score = jnp.where((jj == 0) | (jj == t_blk) | (jj == t_blk - 1), FORCED_SCORE, score)
    score = jnp.where(jj > t_blk, -1.0, score)
    _, sel = lax.top_k(score, n_sel)

    k_s, v_s = kv_slc
    k_blk = k_s.reshape(B, G, n_blk, SLC_BLOCK, d)
    v_blk = v_s.reshape(B, G, n_blk, SLC_BLOCK, d)
    bi = jnp.arange(B)[:, None, None, None]
    gi = jnp.arange(G)[None, :, None, None]
    C = SLC_QUERY_CHUNK
    n_keys = n_sel * SLC_BLOCK

    def chunk(i):
        start = i * C
        q_c = lax.dynamic_slice_in_dim(q, start, C, axis=3)
        sel_c = lax.dynamic_slice_in_dim(sel, start, C, axis=2)
        kg = k_blk[bi, gi, sel_c].reshape(B, G, C, n_keys, d)
        vg = v_blk[bi, gi, sel_c].reshape(B, G, C, n_keys, d)
        kpos = (sel_c[..., None] * SLC_BLOCK + jnp.arange(SLC_BLOCK)).reshape(B, G, C, n_keys)
        dist = ((start + jnp.arange(C))[None, None, :, None] - kpos)[:, :, None]
        s = jnp.einsum('bgrcd,bgckd->bgrck', q_c, kg, preferred_element_type=jnp.float32) * scale
        s = jnp.where(dist >= 0, s - sl * dist, NEG_INF)
        a = jax.nn.softmax(s, axis=-1)
        return jnp.einsum('bgrck,bgckd->bgrcd', a.astype(vg.dtype), vg)

    o_slc = lax.map(chunk, jnp.arange(S // C))
    o_slc = jnp.moveaxis(o_slc, 0, 4).reshape(B, G, R, S, d)

    k_w, v_w = kv_win
    o_win = banded_attention(q, k_w, v_w, slopes, NSA_WINDOW)

    return gates[0] * o_cmp + gates[1] * o_slc + gates[2] * o_win


def _kv_heads(t, groups):
    B, S, _ = t.shape
    return t.reshape(B, S, 2, groups, HEAD_DIM).transpose(2, 0, 3, 1, 4)


def token_mixing(x, w_in, diff_lam, diff_subln_g, cmp_pos, cmp_w1, cmp_w2, sinks,
                 w_branch, w_out, lambda_init):
    B, S, _ = x.shape
    d = HEAD_DIM
    (a_q, a_k, a_v, b_q, b_kvc, b_kvs, b_kvw, b_g,
     c_q, c_k, c_v, m_g) = jnp.split(x @ w_in, _split_offsets(), axis=-1)

    qa = a_q.reshape(B, S, DIFF_HEADS, 2, d).transpose(0, 2, 1, 3, 4)
    ka = a_k.reshape(B, S, DIFF_HEADS, 2, d).transpose(0, 2, 1, 3, 4)
    va = a_v.reshape(B, S, DIFF_HEADS, DIFF_V_DIM).transpose(0, 2, 1, 3)
    o_a = diff_attention(qa, ka, va, diff_lam, diff_subln_g, lambda_init)
    o_a = o_a.transpose(0, 2, 1, 3).reshape(B, S, BRANCH_WIDTH)

    G, R = NSA_KV_GROUPS, NSA_HEADS // NSA_KV_GROUPS
    qb = b_q.reshape(B, S, G, R, d).transpose(0, 2, 3, 1, 4)
    gb = jax.nn.sigmoid(b_g.reshape(B, S, 3, G, R)).transpose(2, 0, 3, 4, 1)[..., None]
    o_b = nsa_attention(qb, _kv_heads(b_kvc, G), _kv_heads(b_kvs, G), _kv_heads(b_kvw, G),
                        gb, cmp_pos, cmp_w1, cmp_w2)
    o_b = o_b.transpose(0, 3, 1, 2, 4).reshape(B, S, BRANCH_WIDTH)

    Gc, Rc = SWA_KV_HEADS, SWA_HEADS // SWA_KV_HEADS
    qc = c_q.reshape(B, S, Gc, Rc, d).transpose(0, 2, 3, 1, 4)
    kc = c_k.reshape(B, S, Gc, d).transpose(0, 2, 1, 3)
    vc = c_v.reshape(B, S, Gc, d).transpose(0, 2, 1, 3)
    o_c = banded_attention(qc, kc, vc, alibi_slopes(SWA_HEADS), SWA_WINDOW, sinks)
    o_c = o_c.transpose(0, 3, 1, 2, 4).reshape(B, S, BRANCH_WIDTH)

    y = jnp.einsum('bsnc,ncd->bsnd', jnp.stack([o_a, o_b, o_c], axis=2), w_branch)
    g = jax.nn.sigmoid(m_g.reshape(B, S, N_BRANCHES, D_MODEL))
    return jnp.einsum('bsnd,bsnd->bsd', g, y) @ w_out


def setup_inputs(seed: int = 0) -> dict:
    key = jax.random.key(seed)
    ks = jax.random.split(key, 17)
    f32 = jnp.float32

    def nrm(k, shape, scale):
        return jax.random.normal(k, shape, f32) * scale

    return {
        'x': nrm(ks[0], (BATCH, SEQ, D_MODEL), 1.0),
        'p': nrm(ks[1], (DEPTH, BATCH, SEQ, PLE_DIM), 1.0),
        'ffn_w_in': nrm(ks[2], (DEPTH, 2, D_MODEL, 2 * D_FF), D_MODEL ** -0.5),
        'ffn_w_out': nrm(ks[3], (DEPTH, 2, D_FF, D_MODEL), BETA * D_FF ** -0.5),
        'ln_g': 1.0 + nrm(ks[4], (DEPTH, 3, D_MODEL), 0.05),
        'ln_b': nrm(ks[5], (DEPTH, 3, D_MODEL), 0.02),
        'w_in': nrm(ks[6], (DEPTH, D_MODEL, D_IN), D_MODEL ** -0.5),
        'diff_lam': nrm(ks[7], (DEPTH, 4, HEAD_DIM), 0.1),
        'diff_subln_g': 1.0 + nrm(ks[8], (DEPTH, DIFF_V_DIM), 0.05),
        'nsa_cmp_pos': nrm(ks[9], (DEPTH, 2, CMP_LEN, HEAD_DIM), 0.1),
        'nsa_cmp_w1': nrm(ks[10], (DEPTH, 2, CMP_LEN * HEAD_DIM, CMP_HIDDEN), (CMP_LEN * HEAD_DIM) ** -0.5),
        'nsa_cmp_w2': nrm(ks[11], (DEPTH, 2, CMP_HIDDEN, HEAD_DIM), CMP_HIDDEN ** -0.5),
        'swa_sinks': nrm(ks[12], (DEPTH, SWA_HEADS), 0.5),
        'w_branch': nrm(ks[13], (DEPTH, N_BRANCHES, BRANCH_WIDTH, D_MODEL), BETA * BRANCH_WIDTH ** -0.5),
        'w_out': nrm(ks[14], (DEPTH, D_MODEL, D_MODEL), BETA * D_MODEL ** -0.5),
        'ple_w_in': nrm(ks[15], (DEPTH, PLE_DIM, D_MODEL), BETA * PLE_DIM ** -0.5),
        'ple_w_gate': nrm(ks[16], (DEPTH, D_MODEL, D_MODEL), D_MODEL ** -0.5),
    }


def reference(x, p, ffn_w_in, ffn_w_out, ln_g, ln_b, w_in, diff_lam, diff_subln_g,
              nsa_cmp_pos, nsa_cmp_w1, nsa_cmp_w2, swa_sinks, w_branch, w_out,
              ple_w_in, ple_w_gate):
    for i in range(DEPTH):
        lambda_init = 0.8 - 0.6 * math.exp(-0.3 * i)
        x = layer_norm(ALPHA * x + 0.5 * swiglu(x, ffn_w_in[i, 0], ffn_w_out[i, 0]),
                       ln_g[i, 0], ln_b[i, 0])
        mix = token_mixing(x, w_in[i], diff_lam[i], diff_subln_g[i], nsa_cmp_pos[i],
                           nsa_cmp_w1[i], nsa_cmp_w2[i], swa_sinks[i], w_branch[i],
                           w_out[i], lambda_init)
        x = layer_norm(ALPHA * x + mix, ln_g[i, 1], ln_b[i, 1])
        h = ALPHA * x + 0.5 * swiglu(x, ffn_w_in[i, 1], ffn_w_out[i, 1])
        h = h + jax.nn.sigmoid(h @ ple_w_gate[i]) * (p[i] @ ple_w_in[i])
        x = layer_norm(h, ln_g[i, 2], ln_b[i, 2])
    return x
```

```python
import functools
import math

import jax
import jax.numpy as jnp
from jax import lax
from jax.experimental import pallas as pl
from jax.experimental.pallas import tpu as pltpu

F32 = jnp.float32
BF16 = jnp.bfloat16

D_MODEL = 1024
DEPTH = 2
HEAD_DIM = 64
DIFF_HEADS = 4
NSA_HEADS = 8
NSA_KV_GROUPS = 2
CMP_LEN = 32
CMP_STRIDE = 16
CMP_HIDDEN = 256
SLC_BLOCK = 64
SLC_TOPK = 16
NSA_WINDOW = 512
FORCED_SCORE = 1e4
SWA_HEADS = 8
SWA_KV_HEADS = 2
SWA_WINDOW = 128
QUERY_BLOCK = 128
SLC_QUERY_CHUNK = 32
N_BRANCHES = 3
BRANCH_WIDTH = 512
D_FF = 2816
PLE_DIM = 256
LN_EPS = 1e-5
SUBLN_EPS = 1e-5
ALPHA = (2 * DEPTH) ** 0.25
NEG_INF = -1e30
QK_SCALE = HEAD_DIM ** -0.5

LANES = 128
VMEM_LIMIT = 56 * 1024 * 1024

BLK_AQ, BLK_AK, BLK_AV = 0, 4, 8
BLK_BQ, BLK_CQ = 12, 16
BLK_SLC, BLK_WIN, BLK_SWA = 20, 24, 28
N_PROJ_BF = 32 * LANES
N_PROJ_F32 = 3 * LANES


def _cparams(sem):
    return pltpu.CompilerParams(dimension_semantics=sem, vmem_limit_bytes=VMEM_LIMIT)


def _layer_norm(h, g, b):
    mu = jnp.mean(h, axis=-1, keepdims=True)
    d = h - mu
    var = jnp.mean(d * d, axis=-1, keepdims=True)
    return d * lax.rsqrt(var + LN_EPS) * g + b


def _dot(a, b):
    return jnp.dot(a, b, preferred_element_type=F32)


def _dot_nt(a, b):
    return lax.dot_general(a, b, (((1,), (1,)), ((), ())), preferred_element_type=F32)


def _ffn_kernel(*refs, n_ff, ple):
    if ple:
        (x_ref, wg_ref, wu_ref, wo_ref, g_ref, b_ref, p_ref, pwg_ref, pwi_ref,
         o_ref, acc_ref, xb_ref) = refs
    else:
        x_ref, wg_ref, wu_ref, wo_ref, g_ref, b_ref, o_ref, acc_ref, xb_ref = refs
    k = pl.program_id(1)

    @pl.when(k == 0)
    def _():
        acc_ref[...] = jnp.zeros_like(acc_ref)
        xb_ref[...] = x_ref[...].astype(BF16)

    xb = xb_ref[...]
    gate = _dot(xb, wg_ref[...])
    up = _dot(xb, wu_ref[...])
    hid = (gate * jax.nn.sigmoid(gate)) * up
    acc_ref[...] += _dot(hid.astype(BF16), wo_ref[...])

    @pl.when(k == n_ff - 1)
    def _():
        h = ALPHA * x_ref[...] + 0.5 * acc_ref[...]
        if ple:
            gp = jax.nn.sigmoid(_dot(h.astype(BF16), pwg_ref[...]))
            h = h + gp * _dot(p_ref[...].astype(BF16), pwi_ref[...])
        o_ref[...] = _layer_norm(h, g_ref[...], b_ref[...])


def _ffn(x, w_in, w_out, ln_g, ln_b, ple_args=None, *, tm=512, tf=256):
    T, D = x.shape
    n_ff = D_FF // tf
    ple = ple_args is not None
    in_specs = [
        pl.BlockSpec((tm, D), lambda i, k: (i, 0)),
        pl.BlockSpec((D, tf), lambda i, k: (0, k)),
        pl.BlockSpec((D, tf), lambda i, k: (0, n_ff + k)),
        pl.BlockSpec((tf, D), lambda i, k: (k, 0)),
        pl.BlockSpec((1, D), lambda i, k: (0, 0)),
        pl.BlockSpec((1, D), lambda i, k: (0, 0)),
    ]
    args = [x, w_in, w_in, w_out, ln_g, ln_b]
    if ple:
        p, pwg, pwi = ple_args
        in_specs += [
            pl.BlockSpec((tm, PLE_DIM), lambda i, k: (i, 0)),
            pl.BlockSpec((D, D), lambda i, k: (0, 0)),
            pl.BlockSpec((PLE_DIM, D), lambda i, k: (0, 0)),
        ]
        args += [p, pwg, pwi]
    return pl.pallas_call(
        functools.partial(_ffn_kernel, n_ff=n_ff, ple=ple),
        grid=(T // tm, n_ff),
        in_specs=in_specs,
        out_specs=pl.BlockSpec((tm, D), lambda i, k: (i, 0)),
        out_shape=jax.ShapeDtypeStruct((T, D), F32),
        scratch_shapes=[pltpu.VMEM((tm, D), F32), pltpu.VMEM((tm, D), BF16)],
        compiler_params=_cparams(("parallel", "arbitrary")),
        name="ffn_ple" if ple else "ffn",
    )(*args)


def _proj_kernel(x_ref, w_ref, o_ref, xb_ref):
    @pl.when(pl.program_id(1) == 0)
    def _():
        xb_ref[...] = x_ref[...].astype(BF16)

    o_ref[...] = _dot(xb_ref[...], w_ref[...]).astype(o_ref.dtype)


def _proj(x, w, out_dtype, *, tm=512, tn=512, name="proj"):
    T, K = x.shape
    N = w.shape[1]
    tn = min(tn, N)
    return pl.pallas_call(
        _proj_kernel,
        grid=(T // tm, N // tn),
        in_specs=[pl.BlockSpec((tm, K), lambda i, j: (i, 0)),
                  pl.BlockSpec((K, tn), lambda i, j: (0, j))],
        out_specs=pl.BlockSpec((tm, tn), lambda i, j: (i, j)),
        out_shape=jax.ShapeDtypeStruct((T, N), out_dtype),
        scratch_shapes=[pltpu.VMEM((tm, K), BF16)],
        compiler_params=_cparams(("parallel", "arbitrary")),
        name=name,
    )(x, w)


def _compress_kernel(r_ref, pos_ref, w1_ref, w2_ref, o_ref):
    r = r_ref[...]
    half = CMP_STRIDE * HEAD_DIM
    lo = _dot((r + pos_ref[0:1, :]).astype(BF16), w1_ref[0:half, :])
    hi = _dot((r + pos_ref[1:2, :]).astype(BF16), w1_ref[half:2 * half, :])
    pre = lo + pltpu.roll(hi, hi.shape[0] - 1, 0)
    hid = jax.nn.gelu(pre, approximate=True)
    o_ref[...] = _dot(hid.astype(BF16), w2_ref[...]).astype(o_ref.dtype)


def _compress(r, pos, w1, w2dup, *, tm=512):
    _, M, F = r.shape
    tm = min(tm, M)
    return pl.pallas_call(
        _compress_kernel,
        grid=(2, M // tm),
        in_specs=[pl.BlockSpec((None, tm, F), lambda n, i: (n, i, 0)),
                  pl.BlockSpec((None, 2, F), lambda n, i: (n, 0, 0)),
                  pl.BlockSpec((None, 2 * F, CMP_HIDDEN), lambda n, i: (n, 0, 0)),
                  pl.BlockSpec((None, CMP_HIDDEN, LANES), lambda n, i: (n, 0, 0))],
        out_specs=pl.BlockSpec((None, tm, LANES), lambda n, i: (n, i, 0)),
        out_shape=jax.ShapeDtypeStruct((2, M, LANES), BF16),
        compiler_params=_cparams(("parallel", "parallel")),
        name="nsa_compress",
    )(r, pos, w1, w2dup)


def _stack_heads(q_tiles):
    tq = q_tiles[0].shape[0]
    lane = lax.broadcasted_iota(jnp.int32, (tq, LANES), 1)
    lo = lane < HEAD_DIM
    zero = jnp.zeros((tq, LANES), BF16)
    parts = []
    for q in q_tiles:
        qs = q * jnp.asarray(QK_SCALE, BF16)
        parts.append(jnp.where(lo, qs, zero))
        parts.append(jnp.where(lo, zero, qs))
    return jnp.concatenate(parts, axis=0)


def _log2(n):
    assert n > 0 and n & (n - 1) == 0, n
    return n.bit_length() - 1


def _head_column(values, nh, tq):
    head = lax.broadcasted_iota(jnp.int32, (nh * tq, 1), 0) >> _log2(tq)
    col = jnp.zeros((nh * tq, 1), F32)
    for h in range(nh):
        col = jnp.where(head == h, values[h], col)
    return col


def _online_softmax_step(s, v, m, l, acc):
    m_new = jnp.maximum(m, jnp.max(s, axis=-1, keepdims=True))
    alpha = jnp.exp(m - m_new)
    p = jnp.exp(s - m_new)
    l = alpha * l + jnp.sum(p, axis=-1, keepdims=True)
    acc = alpha * acc + _dot(p.astype(BF16), v)
    return m_new, l, acc


def _attn_kernel(*refs, mode, tq, tk, window, use_sel, use_sink, heads_per_step,
                 lambda_init):
    refs = list(refs)
    slopes_ref = refs.pop(0)
    sinks_ref = refs.pop(0) if use_sink else None
    q_ref, k_ref, v_ref = refs.pop(0), refs.pop(0), refs.pop(0)
    sel_ref = refs.pop(0) if use_sel else None
    if mode == "diff":
        lam_ref, subg_ref = refs.pop(0), refs.pop(0)
    o_ref = refs.pop(0)

    hg = pl.program_id(1)
    i = pl.program_id(2)
    nh = 2 if mode == "diff" else heads_per_step
    R = nh * tq

    if mode == "diff":
        qs = _stack_heads([q_ref[...]])
        slope_col = jnp.full((R, 1), slopes_ref[hg], F32)
    else:
        qs = _stack_heads([q_ref[:, p * LANES:(p + 1) * LANES] for p in range(nh // 2)])
        slope_col = _head_column([slopes_ref[hg * nh + h] for h in range(nh)], nh, tq)

    row = lax.broadcasted_iota(jnp.int32, (R, tk), 0) & (tq - 1)
    col = lax.broadcasted_iota(jnp.int32, (R, tk), 1)
    d0 = row - col

    if use_sel:
        sel = sel_ref[...]
        blk_of_lane = lax.broadcasted_iota(jnp.int32, (LANES, tk), 0)
        key_blk0 = lax.broadcasted_iota(jnp.int32, (LANES, tk), 1) >> _log2(SLC_BLOCK)

    def block(j, carry):
        m, l, acc = carry
        start = pl.multiple_of(j * tk, tk)
        k = k_ref[pl.ds(start, tk), :]
        v = v_ref[pl.ds(start, tk), :]
        s = _dot_nt(qs, k)
        dist = d0 + (i * tq - j * tk)
        valid = dist >= 0
        if window is not None:
            valid = valid & (dist < window)
        if use_sel:
            expand = (blk_of_lane == key_blk0 + j * (tk // SLC_BLOCK)).astype(BF16)
            chosen = _dot(sel, expand) > 0.5
            valid = valid & jnp.concatenate([chosen] * nh, axis=0)
        s = jnp.where(valid, s - slope_col * dist.astype(F32), NEG_INF)
        return _online_softmax_step(s, v, m, l, acc)

    hi = (i * tq) // tk + tq // tk
    lo = 0 if window is None else jnp.maximum((i * tq - (window - 1)) // tk, 0)
    init = (jnp.full((R, 1), NEG_INF, F32), jnp.zeros((R, 1), F32), jnp.zeros((R, LANES), F32))
    m, l, acc = lax.fori_loop(lo, hi, block, init)

    if use_sink:
        sink_col = _head_column([sinks_ref[hg * nh + h] for h in range(nh)], nh, tq)
        m_f = jnp.maximum(m, sink_col)
        shrink = jnp.exp(m - m_f)
        l = l * shrink + jnp.exp(sink_col - m_f)
        acc = acc * shrink

    o = acc / l
    if mode == "diff":
        lam = lam_ref[...]
        lam_full = (jnp.exp(jnp.sum(lam[0:1, :] * lam[1:2, :], axis=-1, keepdims=True))
                    - jnp.exp(jnp.sum(lam[2:3, :] * lam[3:4, :], axis=-1, keepdims=True))
                    + lambda_init)
        od = o[0:tq, :] - lam_full * o[tq:2 * tq, :]
        od = od * lax.rsqrt(jnp.mean(od * od, axis=-1, keepdims=True) + SUBLN_EPS) * subg_ref[...]
        o_ref[...] = (od * (1.0 - lambda_init)).astype(o_ref.dtype)
    else:
        lane = lax.broadcasted_iota(jnp.int32, (tq, LANES), 1)
        lo_half = lane < HEAD_DIM
        for p in range(nh // 2):
            pair = jnp.where(lo_half, o[(2 * p) * tq:(2 * p + 1) * tq, :],
                             o[(2 * p + 1) * tq:(2 * p + 2) * tq, :])
            o_ref[:, p * LANES:(p + 1) * LANES] = pair.astype(o_ref.dtype)


def _smem_spec():
    return pl.BlockSpec(memory_space=pltpu.SMEM)


def _diff_attn(proj, slopes, lam, subg, B, S, lambda_init, *, tq=256, tk=256):
    nq = S // tq
    kern = functools.partial(_attn_kernel, mode="diff", tq=tq, tk=tk, window=None,
                             use_sel=False, use_sink=False, heads_per_step=2,
                             lambda_init=lambda_init)
    return pl.pallas_call(
        kern,
        grid=(B, DIFF_HEADS, nq),
        in_specs=[_smem_spec(),
                  pl.BlockSpec((tq, LANES), lambda b, h, i: (b * nq + i, BLK_AQ + h)),
                  pl.BlockSpec((S, LANES), lambda b, h, i: (b, BLK_AK + h)),
                  pl.BlockSpec((S, LANES), lambda b, h, i: (b, BLK_AV + h)),
                  pl.BlockSpec((4, HEAD_DIM), lambda b, h, i: (0, 0)),
                  pl.BlockSpec((1, LANES), lambda b, h, i: (0, 0))],
        out_specs=pl.BlockSpec((tq, LANES), lambda b, h, i: (b * nq + i, h)),
        out_shape=jax.ShapeDtypeStruct((B * S, BRANCH_WIDTH), BF16),
        compiler_params=_cparams(("parallel", "parallel", "arbitrary")),
        name="diff_attn",
    )(slopes, proj, proj, proj, lam, subg)


def _gqa_attn(proj, slopes, B, S, *, q_blk, kv_blk, window, sel=None, sinks=None,
              out_dtype, tq=128, tk=128, name):
    nq = S // tq
    G, nh = 2, 4
    kern = functools.partial(_attn_kernel, mode="gqa", tq=tq, tk=tk, window=window,
                             use_sel=sel is not None, use_sink=sinks is not None,
                             heads_per_step=nh, lambda_init=0.0)
    in_specs = [_smem_spec()]
    args = [slopes]
    if sinks is not None:
        in_specs.append(_smem_spec())
        args.append(sinks)
    in_specs += [
        pl.BlockSpec((tq, 2 * LANES), lambda b, g, i: (b * nq + i, q_blk // 2 + g)),
        pl.BlockSpec((S, LANES), lambda b, g, i: (b, kv_blk + g)),
        pl.BlockSpec((S, LANES), lambda b, g, i: (b, kv_blk + 2 + g)),
    ]
    args += [proj, proj, proj]
    if sel is not None:
        in_specs.append(pl.BlockSpec((tq, LANES), lambda b, g, i: (b * nq + i, g)))
        args.append(sel)
    return pl.pallas_call(
        kern,
        grid=(B, G, nq),
        in_specs=in_specs,
        out_specs=pl.BlockSpec((tq, 2 * LANES), lambda b, g, i: (b * nq + i, g)),
        out_shape=jax.ShapeDtypeStruct((B * S, BRANCH_WIDTH), out_dtype),
        compiler_params=_cparams(("parallel", "parallel", "arbitrary")),
        name=name,
    )(*args)


def _cmp_attn_kernel(slopes_ref, q_ref, kc_ref, vc_ref, o_ref, sel_ref, *, tq, n_cmp, n_blk, n_sel):
    g = pl.program_id(1)
    i = pl.program_id(2)
    nh = NSA_HEADS // NSA_KV_GROUPS
    R = nh * tq
    qs = _stack_heads([q_ref[:, p * LANES:(p + 1) * LANES] for p in range(nh // 2)])
    slope_col = _head_column([slopes_ref[g * nh + h] for h in range(nh)], nh, tq)

    t = i * tq + (lax.broadcasted_iota(jnp.int32, (R, LANES), 0) & (tq - 1))
    c = lax.broadcasted_iota(jnp.int32, (R, LANES), 1)
    dist = t - (c * CMP_STRIDE + (CMP_LEN - 1))
    valid = (dist >= 0) & (c < n_cmp)
    s = _dot_nt(qs, kc_ref[...])
    s = jnp.where(valid, s - slope_col * dist.astype(F32), NEG_INF)
    m = jnp.max(s, axis=-1, keepdims=True)
    e = jnp.exp(s - m)
    p_c = e / jnp.sum(e, axis=-1, keepdims=True)
    any_valid = (t[:, 0:1] >= CMP_LEN - 1).astype(F32)
    p_c = p_c * any_valid
    o = _dot(p_c.astype(BF16), vc_ref[...])
    lane = lax.broadcasted_iota(jnp.int32, (tq, LANES), 1)
    lo_half = lane < HEAD_DIM
    for p in range(nh // 2):
        pair = jnp.where(lo_half, o[(2 * p) * tq:(2 * p + 1) * tq, :],
                         o[(2 * p + 1) * tq:(2 * p + 2) * tq, :])
        o_ref[:, p * LANES:(p + 1) * LANES] = pair.astype(o_ref.dtype)

    p_sum = p_c[0:tq, :]
    for h in range(1, nh):
        p_sum = p_sum + p_c[h * tq:(h + 1) * tq, :]
    jj = lax.broadcasted_iota(jnp.int32, (n_blk, LANES), 0)
    cc = lax.broadcasted_iota(jnp.int32, (n_blk, LANES), 1)
    overlap = (jnp.minimum(cc * CMP_STRIDE + CMP_LEN, jj * SLC_BLOCK + SLC_BLOCK)
               - jnp.maximum(cc * CMP_STRIDE, jj * SLC_BLOCK))
    overlap = jnp.where(cc < n_cmp, jnp.maximum(overlap, 0), 0)
    w_t = (overlap.astype(F32) * (1.0 / CMP_LEN)).astype(BF16)
    hi = p_sum.astype(BF16)
    rem = p_sum - hi.astype(F32)
    mid = rem.astype(BF16)
    low = (rem - mid.astype(F32)).astype(BF16)
    score = _dot_nt(w_t, hi) + _dot_nt(w_t, mid) + _dot_nt(w_t, low)

    tpos = i * tq + lax.broadcasted_iota(jnp.int32, (n_blk, tq), 1)
    t_blk = tpos >> _log2(SLC_BLOCK)
    jb = lax.broadcasted_iota(jnp.int32, (n_blk, tq), 0)
    score = jnp.where((jb == 0) | (jb == t_blk) | (jb == t_blk - 1), FORCED_SCORE, score)
    score = jnp.where(jb > t_blk, -1.0, score)
    rank = jnp.zeros((n_blk, tq), F32)
    for kk in range(n_blk):
        row = score[kk:kk + 1, :]
        ahead = (row > score) | ((row == score) & (kk < jb))
        rank = rank + ahead.astype(F32)
    sel_t = (rank < n_sel).astype(BF16)
    sel_t = jnp.concatenate([sel_t, jnp.zeros((LANES - n_blk, tq), BF16)], axis=0)
    eye = (lax.broadcasted_iota(jnp.int32, (tq, tq), 0)
           == lax.broadcasted_iota(jnp.int32, (tq, tq), 1)).astype(BF16)
    sel_ref[...] = _dot_nt(eye, sel_t).astype(sel_ref.dtype)


def _cmp_attn(proj, kvc, slopes, B, S, *, tq=256):
    nq = S // tq
    G = NSA_KV_GROUPS
    n_cmp = (S - CMP_LEN) // CMP_STRIDE + 1
    n_blk = S // SLC_BLOCK
    n_sel = min(SLC_TOPK, n_blk)
    kern = functools.partial(_cmp_attn_kernel, tq=tq, n_cmp=n_cmp, n_blk=n_blk, n_sel=n_sel)
    return pl.pallas_call(
        kern,
        grid=(B, G, nq),
        in_specs=[_smem_spec(),
                  pl.BlockSpec((tq, 2 * LANES), lambda b, g, i: (b * nq + i, BLK_BQ // 2 + g)),
                  pl.BlockSpec((None, LANES, LANES), lambda b, g, i: (0, b * G + g, 0)),
                  pl.BlockSpec((None, LANES, LANES), lambda b, g, i: (1, b * G + g, 0))],
        out_specs=[pl.BlockSpec((tq, 2 * LANES), lambda b, g, i: (b * nq + i, g)),
                   pl.BlockSpec((tq, LANES), lambda b, g, i: (b * nq + i, g))],
        out_shape=[jax.ShapeDtypeStruct((B * S, BRANCH_WIDTH), F32),
                   jax.ShapeDtypeStruct((B * S, G * LANES), BF16)],
        compiler_params=_cparams(("parallel", "parallel", "parallel")),
        name="nsa_cmp_attn",
    )(slopes, proj, kvc, kvc)


def _interleave_matrix(tm, n_outer):
    e_cnt = tm // n_outer
    r_out = lax.broadcasted_iota(jnp.int32, (tm, tm), 0)
    r_in = lax.broadcasted_iota(jnp.int32, (tm, tm), 1)
    outer = r_in >> _log2(e_cnt)
    inner = r_in & (e_cnt - 1)
    return (r_out == inner * n_outer + outer).astype(BF16)


def _permute_rows(pm, x):
    if x.dtype == BF16:
        return _dot(pm, x)
    hi = x.astype(BF16)
    rem = x - hi.astype(F32)
    mid = rem.astype(BF16)
    low = (rem - mid.astype(F32)).astype(BF16)
    return _dot(pm, hi) + _dot(pm, mid) + _dot(pm, low)


def _merge_kernel(x_ref, oa_ref, ocmp_ref, oslc_ref, owin_ref, oc_ref, gts_ref,
                  wmg_ref, wbr_ref, wout_ref, g_ref, b_ref, o_ref):
    tm = x_ref.shape[0]
    x = x_ref[...]
    xb = x.astype(BF16)
    p_slc = _interleave_matrix(tm, oslc_ref.shape[0])
    p_band = _interleave_matrix(tm, owin_ref.shape[0])
    o_slc = _permute_rows(p_slc, oslc_ref[...].reshape(tm, BRANCH_WIDTH))
    o_win = _permute_rows(p_band, owin_ref[...].reshape(tm, BRANCH_WIDTH))
    o_c = _permute_rows(p_band, oc_ref[...].reshape(tm, BRANCH_WIDTH)).astype(BF16)
    gates = jax.nn.sigmoid(gts_ref[...])
    lane = lax.broadcasted_iota(jnp.int32, (tm, LANES), 1)
    lo_half = lane < HEAD_DIM
    nsa = (ocmp_ref[...], o_slc, o_win)
    ob_tiles = []
    for pair in range(NSA_HEADS // 2):
        acc = jnp.zeros((tm, LANES), F32)
        for n in range(3):
            c0 = n * NSA_HEADS + 2 * pair
            gate = jnp.where(lo_half, gates[:, c0:c0 + 1], gates[:, c0 + 1:c0 + 2])
            acc = acc + gate * nsa[n][:, pair * LANES:(pair + 1) * LANES]
        ob_tiles.append(acc.astype(BF16))
    ob = jnp.concatenate(ob_tiles, axis=1)
    branches = (oa_ref[...], ob, o_c)
    y = jnp.zeros((tm, D_MODEL), F32)
    for n in range(N_BRANCHES):
        mg = jax.nn.sigmoid(_dot(xb, wmg_ref[:, n * D_MODEL:(n + 1) * D_MODEL]))
        y = y + mg * _dot(branches[n], wbr_ref[n])
    mix = _dot(y.astype(BF16), wout_ref[...])
    o_ref[...] = _layer_norm(ALPHA * x + mix, g_ref[...], b_ref[...])


def _merge(x, o_a, o_cmp, o_slc, o_win, o_c, projf, w_mg, w_br, w_out, ln_g, ln_b, B, S,
           *, tm=512):
    T, D = x.shape
    nt = S // tm
    n_slc, n_band = S // SLC_QUERY_CHUNK, S // QUERY_BLOCK
    o_slc = o_slc.reshape(B, n_slc, SLC_QUERY_CHUNK, BRANCH_WIDTH)
    o_win = o_win.reshape(B, n_band, QUERY_BLOCK, BRANCH_WIDTH)
    o_c = o_c.reshape(B, n_band, QUERY_BLOCK, BRANCH_WIDTH)
    row = lambda b, a: (b * nt + a, 0)
    const2 = lambda b, a: (0, 0)
    chunked = lambda b, a: (b, 0, a, 0)
    return pl.pallas_call(
        _merge_kernel,
        grid=(B, nt),
        in_specs=[pl.BlockSpec((tm, D), row),
                  pl.BlockSpec((tm, BRANCH_WIDTH), row),
                  pl.BlockSpec((tm, BRANCH_WIDTH), row),
                  pl.BlockSpec((None, n_slc, tm // n_slc, BRANCH_WIDTH), chunked),
                  pl.BlockSpec((None, n_band, tm // n_band, BRANCH_WIDTH), chunked),
                  pl.BlockSpec((None, n_band, tm // n_band, BRANCH_WIDTH), chunked),
                  pl.BlockSpec((tm, LANES), lambda b, a: (b * nt + a, 2)),
                  pl.BlockSpec((D, N_BRANCHES * D), const2),
                  pl.BlockSpec((N_BRANCHES, BRANCH_WIDTH, D), lambda b, a: (0, 0, 0)),
                  pl.BlockSpec((D, D), const2),
                  pl.BlockSpec((1, D), const2),
                  pl.BlockSpec((1, D), const2)],
        out_specs=pl.BlockSpec((tm, D), row),
        out_shape=jax.ShapeDtypeStruct((T, D), F32),
        compiler_params=_cparams(("parallel", "parallel")),
        name="merge",
    )(x, o_a, o_cmp, o_slc, o_win, o_c, projf, w_mg, w_br, w_out, ln_g, ln_b)


def _dup_kv(w, groups):
    D = w.shape[0]
    w = w.reshape(D, 2 * groups, 1, HEAD_DIM)
    return jnp.broadcast_to(w, (D, 2 * groups, 2, HEAD_DIM)).reshape(D, 4 * groups * HEAD_DIM)


def _split_w_in(w):
    sizes = (512, 512, 512, 512, 256, 256, 256, 24, 512, 128, 128, 3072)
    offs = [0]
    for n in sizes:
        offs.append(offs[-1] + n)
    (a_q, a_k, a_v, b_q, b_kvc, b_kvs, b_kvw, b_g, c_q, c_k, c_v, m_g) = [
        w[:, offs[n]:offs[n + 1]] for n in range(len(sizes))]
    w_bf = jnp.concatenate(
        [a_q, a_k, a_v, b_q, c_q, _dup_kv(b_kvs, 2), _dup_kv(b_kvw, 2),
         _dup_kv(jnp.concatenate([c_k, c_v], axis=1), 2)], axis=1).astype(BF16)
    pad = jnp.zeros((w.shape[0], N_PROJ_F32 - 256 - 24), w.dtype)
    w_f32 = jnp.concatenate([b_kvc, b_g, pad], axis=1).astype(BF16)
    return w_bf, w_f32, m_g.astype(BF16)


def _alibi_slopes(n_heads):
    return jnp.exp2(-8.0 * jnp.arange(1, n_heads + 1, dtype=F32) / n_heads)


def _token_mixing(x, B, S, w_in, diff_lam, diff_subln_g, cmp_pos, cmp_w1, cmp_w2, sinks,
                  w_branch, w_out, ln_g, ln_b, lambda_init):
    G = NSA_KV_GROUPS
    w_bf, w_f32, w_mg = _split_w_in(w_in)
    proj = _proj(x, w_bf, BF16, name="proj_bf16")
    projf = _proj(x, w_f32, F32, name="proj_f32")

    n_rows = S // CMP_STRIDE
    r = projf[:, :4 * HEAD_DIM].reshape(B, S, 2, G, HEAD_DIM).transpose(2, 0, 3, 1, 4)
    r = r.reshape(2, B * G * n_rows, CMP_STRIDE * HEAD_DIM)
    pos = cmp_pos.reshape(2, 2, CMP_STRIDE * HEAD_DIM)
    w2dup = jnp.concatenate([cmp_w2, cmp_w2], axis=-1).astype(BF16)
    kvc = _compress(r, pos, cmp_w1.astype(BF16), w2dup)

    slopes8 = _alibi_slopes(NSA_HEADS)
    o_a = _diff_attn(proj, _alibi_slopes(DIFF_HEADS), diff_lam, diff_subln_g.reshape(1, LANES),
                     B, S, lambda_init)
    o_cmp, sel = _cmp_attn(proj, kvc, slopes8, B, S)
    o_slc = _gqa_attn(proj, slopes8, B, S, q_blk=BLK_BQ, kv_blk=BLK_SLC, window=None, sel=sel,
                      out_dtype=F32, name="nsa_slc_attn")
    o_win = _gqa_attn(proj, slopes8, B, S, q_blk=BLK_BQ, kv_blk=BLK_WIN, window=NSA_WINDOW,
                      out_dtype=F32, name="nsa_win_attn")
    o_c = _gqa_attn(proj, _alibi_slopes(SWA_HEADS), B, S, q_blk=BLK_CQ, kv_blk=BLK_SWA,
                    window=SWA_WINDOW, sinks=sinks, out_dtype=BF16, name="swa_attn")
    return _merge(x, o_a, o_cmp, o_slc, o_win, o_c, projf, w_mg, w_branch.astype(BF16),
                  w_out.astype(BF16), ln_g, ln_b, B, S)


def kernel(x, p, ffn_w_in, ffn_w_out, ln_g, ln_b, w_in, diff_lam, diff_subln_g, nsa_cmp_pos,
           nsa_cmp_w1, nsa_cmp_w2, swa_sinks, w_branch, w_out, ple_w_in, ple_w_gate):
    B, S, D = x.shape
    T = B * S
    h = x.reshape(T, D)
    for i in range(DEPTH):
        lambda_init = 0.8 - 0.6 * math.exp(-0.3 * i)
        lg = ln_g[i].reshape(3, 1, D)
        lb = ln_b[i].reshape(3, 1, D)
        h = _ffn(h, ffn_w_in[i, 0].astype(BF16), ffn_w_out[i, 0].astype(BF16), lg[0], lb[0])
        h = _token_mixing(h, B, S, w_in[i], diff_lam[i], diff_subln_g[i], nsa_cmp_pos[i],
                          nsa_cmp_w1[i], nsa_cmp_w2[i], swa_sinks[i], w_branch[i], w_out[i],
                          lg[1], lb[1], lambda_init)
        h = _ffn(h, ffn_w_in[i, 1].astype(BF16), ffn_w_out[i, 1].astype(BF16), lg[2], lb[2],
                 ple_args=(p[i].reshape(T, PLE_DIM), ple_w_gate[i].astype(BF16),
                           ple_w_in[i].astype(BF16)))
    return h.reshape(B, S, D)
```

```python
import functools
import math

import jax
import jax.numpy as jnp
from jax import lax
from jax.experimental import pallas as pl
from jax.experimental.pallas import tpu as pltpu

F32 = jnp.float32
BF16 = jnp.bfloat16

D_MODEL = 1024
DEPTH = 2
HEAD_DIM = 64
DIFF_HEADS = 4
NSA_HEADS = 8
NSA_KV_GROUPS = 2
CMP_LEN = 32
CMP_STRIDE = 16
CMP_HIDDEN = 256
SLC_BLOCK = 64
SLC_TOPK = 16
NSA_WINDOW = 512
FORCED_SCORE = 1e4
SWA_HEADS = 8
SWA_KV_HEADS = 2
SWA_WINDOW = 128
QUERY_BLOCK = 128
SLC_QUERY_CHUNK = 32
N_BRANCHES = 3
BRANCH_WIDTH = 512
D_FF = 2816
PLE_DIM = 256
LN_EPS = 1e-5
SUBLN_EPS = 1e-5
ALPHA = (2 * DEPTH) ** 0.25
NEG_INF = -1e30
QK_SCALE = HEAD_DIM ** -0.5

LANES = 128
VMEM_LIMIT = 56 * 1024 * 1024

BLK_AQ, BLK_AK, BLK_AV = 0, 4, 8
BLK_BQ, BLK_CQ = 12, 16
BLK_SLC, BLK_WIN, BLK_SWA = 20, 24, 28
N_PROJ_BF = 32 * LANES
N_PROJ_F32 = 3 * LANES


def _cparams(sem):
    return pltpu.CompilerParams(dimension_semantics=sem, vmem_limit_bytes=VMEM_LIMIT)


def _layer_norm(h, g, b):
    mu = jnp.mean(h, axis=-1, keepdims=True)
    d = h - mu
    var = jnp.mean(d * d, axis=-1, keepdims=True)
    return d * lax.rsqrt(var + LN_EPS) * g + b


def _dot(a, b):
    return jnp.dot(a, b, preferred_element_type=F32)


def _dot_nt(a, b):
    return lax.dot_general(a, b, (((1,), (1,)), ((), ())), preferred_element_type=F32)


def _ffn_kernel(*refs, n_ff, ple):
    if ple:
        (x_ref, wg_ref, wu_ref, wo_ref, g_ref, b_ref, p_ref, pwg_ref, pwi_ref,
         o_ref, acc_ref, xb_ref) = refs
    else:
        x_ref, wg_ref, wu_ref, wo_ref, g_ref, b_ref, o_ref, acc_ref, xb_ref = refs
    k = pl.program_id(1)

    @pl.when(k == 0)
    def _():
        acc_ref[...] = jnp.zeros_like(acc_ref)
        xb_ref[...] = x_ref[...].astype(BF16)

    xb = xb_ref[...]
    gate = _dot(xb, wg_ref[...])
    up = _dot(xb, wu_ref[...])
    hid = (gate * jax.nn.sigmoid(gate)) * up
    acc_ref[...] += _dot(hid.astype(BF16), wo_ref[...])

    @pl.when(k == n_ff - 1)
    def _():
        h = ALPHA * x_ref[...] + 0.5 * acc_ref[...]
        if ple:
            gp = jax.nn.sigmoid(_dot(h.astype(BF16), pwg_ref[...]))
            h = h + gp * _dot(p_ref[...].astype(BF16), pwi_ref[...])
        o_ref[...] = _layer_norm(h, g_ref[...], b_ref[...])


def _ffn(x, w_in, w_out, ln_g, ln_b, ple_args=None, *, tm=512, tf=256):
    T, D = x.shape
    n_ff = D_FF // tf
    ple = ple_args is not None
    in_specs = [
        pl.BlockSpec((tm, D), lambda i, k: (i, 0)),
        pl.BlockSpec((D, tf), lambda i, k: (0, k)),
        pl.BlockSpec((D, tf), lambda i, k: (0, n_ff + k)),
        pl.BlockSpec((tf, D), lambda i, k: (k, 0)),
        pl.BlockSpec((1, D), lambda i, k: (0, 0)),
        pl.BlockSpec((1, D), lambda i, k: (0, 0)),
    ]
    args = [x, w_in, w_in, w_out, ln_g, ln_b]
    if ple:
        p, pwg, pwi = ple_args
        in_specs += [
            pl.BlockSpec((tm, PLE_DIM), lambda i, k: (i, 0)),
            pl.BlockSpec((D, D), lambda i, k: (0, 0)),
            pl.BlockSpec((PLE_DIM, D), lambda i, k: (0, 0)),
        ]
        args += [p, pwg, pwi]
    return pl.pallas_call(
        functools.partial(_ffn_kernel, n_ff=n_ff, ple=ple),
        grid=(T // tm, n_ff),
        in_specs=in_specs,
        out_specs=pl.BlockSpec((tm, D), lambda i, k: (i, 0)),
        out_shape=jax.ShapeDtypeStruct((T, D), F32),
        scratch_shapes=[pltpu.VMEM((tm, D), F32), pltpu.VMEM((tm, D), BF16)],
        compiler_params=_cparams(("parallel", "arbitrary")),
        name="ffn_ple" if ple else "ffn",
    )(*args)


def _proj_kernel(x_ref, w_ref, o_ref, xb_ref):
    @pl.when(pl.program_id(1) == 0)
    def _():
        xb_ref[...] = x_ref[...].astype(BF16)

    o_ref[...] = _dot(xb_ref[...], w_ref[...]).astype(o_ref.dtype)


def _proj(x, w, out_dtype, *, tm=512, tn=512, name="proj"):
    T, K = x.shape
    N = w.shape[1]
    tn = min(tn, N)
    return pl.pallas_call(
        _proj_kernel,
        grid=(T // tm, N // tn),
        in_specs=[pl.BlockSpec((tm, K), lambda i, j: (i, 0)),
                  pl.BlockSpec((K, tn), lambda i, j: (0, j))],
        out_specs=pl.BlockSpec((tm, tn), lambda i, j: (i, j)),
        out_shape=jax.ShapeDtypeStruct((T, N), out_dtype),
        scratch_shapes=[pltpu.VMEM((tm, K), BF16)],
        compiler_params=_cparams(("parallel", "arbitrary")),
        name=name,
    )(x, w)


def _compress_kernel(r_ref, pos_ref, w1_ref, w2_ref, o_ref):
    r = r_ref[...]
    half = CMP_STRIDE * HEAD_DIM
    lo = _dot((r + pos_ref[0:1, :]).astype(BF16), w1_ref[0:half, :])
    hi = _dot((r + pos_ref[1:2, :]).astype(BF16), w1_ref[half:2 * half, :])
    pre = lo + pltpu.roll(hi, hi.shape[0] - 1, 0)
    hid = jax.nn.gelu(pre, approximate=True)
    o_ref[...] = _dot(hid.astype(BF16), w2_ref[...]).astype(o_ref.dtype)


def _compress(r, pos, w1, w2dup, *, tm=512):
    _, M, F = r.shape
    tm = min(tm, M)
    return pl.pallas_call(
        _compress_kernel,
        grid=(2, M // tm),
        in_specs=[pl.BlockSpec((None, tm, F), lambda n, i: (n, i, 0)),
                  pl.BlockSpec((None, 2, F), lambda n, i: (n, 0, 0)),
                  pl.BlockSpec((None, 2 * F, CMP_HIDDEN), lambda n, i: (n, 0, 0)),
                  pl.BlockSpec((None, CMP_HIDDEN, LANES), lambda n, i: (n, 0, 0))],
        out_specs=pl.BlockSpec((None, tm, LANES), lambda n, i: (n, i, 0)),
        out_shape=jax.ShapeDtypeStruct((2, M, LANES), BF16),
        compiler_params=_cparams(("parallel", "parallel")),
        name="nsa_compress",
    )(r, pos, w1, w2dup)


def _log2(n):
    assert n > 0 and n & (n - 1) == 0, n
    return n.bit_length() - 1


def _stack_heads(q_tiles):
    tq = q_tiles[0].shape[0]
    lane = lax.broadcasted_iota(jnp.int32, (tq, LANES), 1)
    lo = lane < HEAD_DIM
    zero = jnp.zeros((tq, LANES), BF16)
    scaled = [q * jnp.asarray(QK_SCALE, BF16) for q in q_tiles]
    parts = [jnp.where(lo, qs, zero) for qs in scaled] + [jnp.where(lo, zero, qs) for qs in scaled]
    return jnp.concatenate(parts, axis=0)


def _stacked_head_order(nh):
    return list(range(0, nh, 2)) + list(range(1, nh, 2))


def _head_column(values, tq):
    nh = len(values)
    head = lax.broadcasted_iota(jnp.int32, (nh * tq, 1), 0) >> _log2(tq)
    col = jnp.zeros((nh * tq, 1), F32)
    for h in range(nh):
        col = jnp.where(head == h, values[h], col)
    return col


KX_ONE_A, KX_COL_HI, KX_COL_LO, KX_ONE_B = 32, 33, 34, 35
POS_SPLIT = 16


def _key_extras(S, tk):
    assert S // SLC_BLOCK <= KX_ONE_A
    key = jnp.arange(S, dtype=jnp.int32)[:, None]
    lane = jnp.arange(LANES, dtype=jnp.int32)[None, :]
    col = key % tk
    x = jnp.where(lane == key // SLC_BLOCK, -NEG_INF, 0.0)
    x = jnp.where((lane == KX_ONE_A) | (lane == KX_ONE_B), 1.0, x)
    x = jnp.where(lane == KX_COL_HI, (col - col % POS_SPLIT).astype(F32), x)
    x = jnp.where(lane == KX_COL_LO, (col % POS_SPLIT).astype(F32), x)
    return x.astype(BF16)


def _attn_kernel(*refs, mode, tq, tk, window, use_sel, use_sink, nh, lambda_init, offsets):
    refs = list(refs)
    slopes_ref = refs.pop(0)
    sinks_ref = refs.pop(0) if use_sink else None
    q_ref, k_ref, v_ref, kx_ref = refs.pop(0), refs.pop(0), refs.pop(0), refs.pop(0)
    sel_ref = refs.pop(0) if use_sel else None
    if mode == "diff":
        lam_ref, subg_ref = refs.pop(0), refs.pop(0)
    o_ref = refs.pop(0)
    mask_ref = refs.pop(0)
    hg = pl.program_id(1)
    i = pl.program_id(2)
    R = nh * tq
    gqa = mode == "gqa"
    ratio = tk // tq

    if gqa:
        order = _stacked_head_order(nh)
        qs = _stack_heads([q_ref[:, p * LANES:(p + 1) * LANES] for p in range(nh // 2)])
        slope_col = _head_column([slopes_ref[hg * nh + h] for h in order], tq)
    else:
        qs = _stack_heads([q_ref[...]])
        slope_col = jnp.full((R, 1), slopes_ref[hg], F32)
    slope_tk = slope_col * float(tk)

    @pl.when(i == 0)
    def _():
        row = lax.broadcasted_iota(jnp.int32, (R, tk), 0) & (tq - 1)
        col = lax.broadcasted_iota(jnp.int32, (R, tk), 1)
        d0 = row - col
        if window is None:
            for a in range(ratio):
                mask_ref[a] = jnp.where(d0 + a * tq >= 0, 0.0, NEG_INF)
        else:
            for n, off in enumerate(offsets):
                dist = d0 + off * tk
                mask_ref[n] = jnp.where((dist >= 0) & (dist < window), 0.0, NEG_INF)

    xl = lax.broadcasted_iota(jnp.int32, (R, LANES), 1)
    xr = lax.broadcasted_iota(jnp.int32, (R, LANES), 0) & (tq - 1)
    xr_lo = xr & (POS_SPLIT - 1)
    qx = jnp.where(xl == KX_ONE_A, -slope_col * (xr - xr_lo).astype(F32), 0.0)
    qx = jnp.where(xl == KX_ONE_B, -slope_col * xr_lo.astype(F32), qx)
    qx = jnp.where((xl == KX_COL_HI) | (xl == KX_COL_LO), slope_col, qx)
    if use_sel:
        n_blk = k_ref.shape[0] // SLC_BLOCK
        not_sel = jnp.concatenate([sel_ref[...].astype(F32) - 1.0] * nh, axis=0)
        qx = jnp.where(xl < n_blk, not_sel, qx)
    qs = jnp.concatenate([qs, qx.astype(BF16)], axis=1)
    lo_half_k = lax.broadcasted_iota(jnp.int32, (tk, LANES), 1) < HEAD_DIM

    def block(j, carry, mask):
        start = pl.multiple_of(j * tk, tk)
        k = k_ref[pl.ds(start, tk), :]
        v = v_ref[pl.ds(start, tk), :]
        kx = kx_ref[pl.ds(start, tk), :]
        s = _dot_nt(qs, jnp.concatenate([k, kx], axis=1))
        if mask is not None:
            s = s + mask
        if gqa:
            m, acc_e, acc_o = carry
        else:
            m, l, acc = carry
        m_prev = m - slope_tk
        m_new = jnp.maximum(m_prev, jnp.max(s, axis=-1, keepdims=True))
        alpha = jnp.exp(m_prev - m_new)
        p32 = jnp.exp(s - m_new)
        p = p32.astype(BF16)
        if gqa:
            one = jnp.ones((tk, LANES), BF16)
            half = R // 2
            acc_e = alpha[:half] * acc_e + _dot(p[:half], jnp.where(lo_half_k, v, one))
            acc_o = alpha[half:] * acc_o + _dot(p[half:], jnp.where(lo_half_k, one, v))
            return m_new, acc_e, acc_o
        l = alpha * l + jnp.sum(p32, axis=-1, keepdims=True)
        acc = alpha * acc + _dot(p, v)
        return m_new, l, acc

    m0 = jnp.full((R, 1), NEG_INF, F32)
    if gqa:
        carry = (m0, jnp.zeros((R // 2, LANES), F32), jnp.zeros((R // 2, LANES), F32))
    else:
        carry = (m0, jnp.zeros((R, 1), F32), jnp.zeros((R, LANES), F32))

    if window is None:
        j_last = i // ratio
        carry = lax.fori_loop(0, j_last, lambda j, c: block(j, c, None), carry)
        carry = block(j_last, carry, mask_ref[i % ratio])
    else:
        for n, off in enumerate(offsets):
            if off == 0:
                carry = block(i, carry, mask_ref[n])
            else:
                carry = lax.cond(i >= off, lambda c, off=off, n=n: block(i - off, c, mask_ref[n]),
                                 lambda c: c, carry)

    if gqa:
        m, acc_e, acc_o = carry
        half = R // 2
        l_e = acc_e[:, HEAD_DIM:HEAD_DIM + 1]
        l_o = acc_o[:, 0:1]
        if use_sink:
            sink_col = _head_column([sinks_ref[hg * nh + h] for h in order], tq)
            m_f = jnp.maximum(m, sink_col)
            shrink = jnp.exp(m - m_f)
            extra = jnp.exp(sink_col - m_f)
            acc_e = acc_e * shrink[:half]
            acc_o = acc_o * shrink[half:]
            l_e = l_e * shrink[:half] + extra[:half]
            l_o = l_o * shrink[half:] + extra[half:]
        o_e = acc_e / l_e
        o_o = acc_o / l_o
        lo_half_q = lax.broadcasted_iota(jnp.int32, (tq, LANES), 1) < HEAD_DIM
        for p_ in range(nh // 2):
            pair = jnp.where(lo_half_q, o_e[p_ * tq:(p_ + 1) * tq, :], o_o[p_ * tq:(p_ + 1) * tq, :])
            o_ref[:, p_ * LANES:(p_ + 1) * LANES] = pair.astype(o_ref.dtype)
    else:
        m, l, acc = carry
        o = acc / l
        lam = lam_ref[...]
        lam_full = (jnp.exp(jnp.sum(lam[0:1, :] * lam[1:2, :], axis=-1, keepdims=True))
                    - jnp.exp(jnp.sum(lam[2:3, :] * lam[3:4, :], axis=-1, keepdims=True))
                    + lambda_init)
        od = o[0:tq, :] - lam_full * o[tq:2 * tq, :]
        od = od * lax.rsqrt(jnp.mean(od * od, axis=-1, keepdims=True) + SUBLN_EPS) * subg_ref[...]
        o_ref[...] = (od * (1.0 - lambda_init)).astype(o_ref.dtype)


def _window_offsets(window, tk):
    return tuple(range((window + tk - 2) // tk, -1, -1))


def _smem_spec():
    return pl.BlockSpec(memory_space=pltpu.SMEM)


def _diff_attn(proj, slopes, lam, subg, B, S, lambda_init, *, tq, tk):
    nq = S // tq
    kern = functools.partial(_attn_kernel, mode="diff", tq=tq, tk=tk, window=None, use_sel=False,
                             use_sink=False, nh=2, lambda_init=lambda_init, offsets=None)
    return pl.pallas_call(
        kern,
        grid=(B, DIFF_HEADS, nq),
        in_specs=[_smem_spec(),
                  pl.BlockSpec((tq, LANES), lambda b, h, i: (b * nq + i, BLK_AQ + h)),
                  pl.BlockSpec((S, LANES), lambda b, h, i: (b, BLK_AK + h)),
                  pl.BlockSpec((S, LANES), lambda b, h, i: (b, BLK_AV + h)),
                  pl.BlockSpec((S, LANES), lambda b, h, i: (0, 0)),
                  pl.BlockSpec((4, HEAD_DIM), lambda b, h, i: (0, 0)),
                  pl.BlockSpec((1, LANES), lambda b, h, i: (0, 0))],
        out_specs=pl.BlockSpec((tq, LANES), lambda b, h, i: (b * nq + i, h)),
        out_shape=jax.ShapeDtypeStruct((B * S, BRANCH_WIDTH), BF16),
        scratch_shapes=[pltpu.VMEM((tk // tq, 2 * tq, tk), F32)],
        compiler_params=_cparams(("parallel", "parallel", "arbitrary")),
        name="diff_attn",
    )(slopes, proj, proj, proj, _key_extras(S, tk), lam, subg)


def _gqa_attn(proj, slopes, B, S, *, q_blk, kv_blk, window, sel=None, sinks=None,
              out_dtype, tq, tk, name):
    nq = S // tq
    G, nh = 2, 4
    if window is None:
        offsets, n_masks = None, tk // tq
    else:
        assert tk == tq
        offsets = _window_offsets(window, tk)
        n_masks = len(offsets)
    kern = functools.partial(_attn_kernel, mode="gqa", tq=tq, tk=tk, window=window,
                             use_sel=sel is not None, use_sink=sinks is not None,
                             nh=nh, lambda_init=0.0, offsets=offsets)
    in_specs = [_smem_spec()]
    args = [slopes]
    if sinks is not None:
        in_specs.append(_smem_spec())
        args.append(sinks)
    in_specs += [
        pl.BlockSpec((tq, 2 * LANES), lambda b, g, i: (b * nq + i, q_blk // 2 + g)),
        pl.BlockSpec((S, LANES), lambda b, g, i: (b, kv_blk + g)),
        pl.BlockSpec((S, LANES), lambda b, g, i: (b, kv_blk + 2 + g)),
        pl.BlockSpec((S, LANES), lambda b, g, i: (0, 0)),
    ]
    args += [proj, proj, proj, _key_extras(S, tk)]
    if sel is not None:
        in_specs.append(pl.BlockSpec((tq, LANES), lambda b, g, i: (b * nq + i, g)))
        args.append(sel)
    return pl.pallas_call(
        kern,
        grid=(B, G, nq),
        in_specs=in_specs,
        out_specs=pl.BlockSpec((tq, 2 * LANES), lambda b, g, i: (b * nq + i, g)),
        out_shape=jax.ShapeDtypeStruct((B * S, BRANCH_WIDTH), out_dtype),
        scratch_shapes=[pltpu.VMEM((n_masks, nh * tq, tk), F32)],
        compiler_params=_cparams(("parallel", "parallel", "arbitrary")),
        name=name,
    )(*args)


def _cmp_attn_kernel(slopes_ref, q_ref, kc_ref, vc_ref, o_ref, sel_ref, *, tq, n_cmp, n_blk, n_sel):
    g = pl.program_id(1)
    i = pl.program_id(2)
    nh = NSA_HEADS // NSA_KV_GROUPS
    R = nh * tq
    qs = _stack_heads([q_ref[:, p * LANES:(p + 1) * LANES] for p in range(nh // 2)])
    slope_col = _head_column([slopes_ref[g * nh + h] for h in _stacked_head_order(nh)], tq)

    t = i * tq + (lax.broadcasted_iota(jnp.int32, (R, LANES), 0) & (tq - 1))
    c = lax.broadcasted_iota(jnp.int32, (R, LANES), 1)
    dist = t - (c * CMP_STRIDE + (CMP_LEN - 1))
    valid = (dist >= 0) & (c < n_cmp)
    s = _dot_nt(qs, kc_ref[...])
    s = jnp.where(valid, s - slope_col * dist.astype(F32), NEG_INF)
    m = jnp.max(s, axis=-1, keepdims=True)
    e = jnp.exp(s - m)
    p_c = e / jnp.sum(e, axis=-1, keepdims=True)
    any_valid = (t[:, 0:1] >= CMP_LEN - 1).astype(F32)
    p_c = p_c * any_valid
    o = _dot(p_c.astype(BF16), vc_ref[...])
    lane = lax.broadcasted_iota(jnp.int32, (tq, LANES), 1)
    lo_half = lane < HEAD_DIM
    for p in range(nh // 2):
        pair = jnp.where(lo_half, o[p * tq:(p + 1) * tq, :],
                         o[(nh // 2 + p) * tq:(nh // 2 + p + 1) * tq, :])
        o_ref[:, p * LANES:(p + 1) * LANES] = pair.astype(o_ref.dtype)

    p_sum = p_c[0:tq, :]
    for h in range(1, nh):
        p_sum = p_sum + p_c[h * tq:(h + 1) * tq, :]
    jj = lax.broadcasted_iota(jnp.int32, (n_blk, LANES), 0)
    cc = lax.broadcasted_iota(jnp.int32, (n_blk, LANES), 1)
    overlap = (jnp.minimum(cc * CMP_STRIDE + CMP_LEN, jj * SLC_BLOCK + SLC_BLOCK)
               - jnp.maximum(cc * CMP_STRIDE, jj * SLC_BLOCK))
    overlap = jnp.where(cc < n_cmp, jnp.maximum(overlap, 0), 0)
    w_t = (overlap.astype(F32) * (1.0 / CMP_LEN)).astype(BF16)
    hi = p_sum.astype(BF16)
    rem = p_sum - hi.astype(F32)
    mid = rem.astype(BF16)
    low = (rem - mid.astype(F32)).astype(BF16)
    score = _dot_nt(w_t, hi) + _dot_nt(w_t, mid) + _dot_nt(w_t, low)

    tpos = i * tq + lax.broadcasted_iota(jnp.int32, (n_blk, tq), 1)
    t_blk = tpos >> _log2(SLC_BLOCK)
    jb = lax.broadcasted_iota(jnp.int32, (n_blk, tq), 0)
    score = jnp.where((jb == 0) | (jb == t_blk) | (jb == t_blk - 1), FORCED_SCORE, score)
    score = jnp.where(jb > t_blk, -1.0, score)
    rank = jnp.zeros((n_blk, tq), F32)
    for kk in range(n_blk):
        row = score[kk:kk + 1, :]
        ahead = (row > score) | ((row == score) & (kk < jb))
        rank = rank + ahead.astype(F32)
    sel_t = (rank < n_sel).astype(BF16)
    sel_t = jnp.concatenate([sel_t, jnp.zeros((LANES - n_blk, tq), BF16)], axis=0)
    eye = (lax.broadcasted_iota(jnp.int32, (tq, tq), 0)
           == lax.broadcasted_iota(jnp.int32, (tq, tq), 1)).astype(BF16)
    sel_ref[...] = _dot_nt(eye, sel_t).astype(sel_ref.dtype)


def _cmp_attn(proj, kvc, slopes, B, S, *, tq=256):
    nq = S // tq
    G = NSA_KV_GROUPS
    n_cmp = (S - CMP_LEN) // CMP_STRIDE + 1
    n_blk = S // SLC_BLOCK
    n_sel = min(SLC_TOPK, n_blk)
    kern = functools.partial(_cmp_attn_kernel, tq=tq, n_cmp=n_cmp, n_blk=n_blk, n_sel=n_sel)
    return pl.pallas_call(
        kern,
        grid=(B, G, nq),
        in_specs=[_smem_spec(),
                  pl.BlockSpec((tq, 2 * LANES), lambda b, g, i: (b * nq + i, BLK_BQ // 2 + g)),
                  pl.BlockSpec((None, LANES, LANES), lambda b, g, i: (0, b * G + g, 0)),
                  pl.BlockSpec((None, LANES, LANES), lambda b, g, i: (1, b * G + g, 0))],
        out_specs=[pl.BlockSpec((tq, 2 * LANES), lambda b, g, i: (b * nq + i, g)),
                   pl.BlockSpec((tq, LANES), lambda b, g, i: (b * nq + i, g))],
        out_shape=[jax.ShapeDtypeStruct((B * S, BRANCH_WIDTH), F32),
                   jax.ShapeDtypeStruct((B * S, G * LANES), BF16)],
        compiler_params=_cparams(("parallel", "parallel", "parallel")),
        name="nsa_cmp_attn",
    )(slopes, proj, kvc, kvc)


def _interleave_matrix(tm, n_outer):
    e_cnt = tm // n_outer
    r_out = lax.broadcasted_iota(jnp.int32, (tm, tm), 0)
    r_in = lax.broadcasted_iota(jnp.int32, (tm, tm), 1)
    outer = r_in >> _log2(e_cnt)
    inner = r_in & (e_cnt - 1)
    return (r_out == inner * n_outer + outer).astype(BF16)


def _permute_rows(pm, x):
    if x.dtype == BF16:
        return _dot(pm, x)
    hi = x.astype(BF16)
    rem = x - hi.astype(F32)
    mid = rem.astype(BF16)
    low = (rem - mid.astype(F32)).astype(BF16)
    return _dot(pm, hi) + _dot(pm, mid) + _dot(pm, low)


def _merge_kernel(x_ref, oa_ref, ocmp_ref, oslc_ref, owin_ref, oc_ref, gts_ref,
                  wmg_ref, wbr_ref, wout_ref, g_ref, b_ref, o_ref):
    tm = x_ref.shape[0]
    x = x_ref[...]
    xb = x.astype(BF16)
    p_slc = _interleave_matrix(tm, oslc_ref.shape[0])
    p_band = _interleave_matrix(tm, owin_ref.shape[0])
    o_slc = _permute_rows(p_slc, oslc_ref[...].reshape(tm, BRANCH_WIDTH))
    o_win = _permute_rows(p_band, owin_ref[...].reshape(tm, BRANCH_WIDTH))
    o_c = _permute_rows(p_band, oc_ref[...].reshape(tm, BRANCH_WIDTH)).astype(BF16)
    gates = jax.nn.sigmoid(gts_ref[...])
    lane = lax.broadcasted_iota(jnp.int32, (tm, LANES), 1)
    lo_half = lane < HEAD_DIM
    nsa = (ocmp_ref[...], o_slc, o_win)
    ob_tiles = []
    for pair in range(NSA_HEADS // 2):
        acc = jnp.zeros((tm, LANES), F32)
        for n in range(3):
            c0 = n * NSA_HEADS + 2 * pair
            gate = jnp.where(lo_half, gates[:, c0:c0 + 1], gates[:, c0 + 1:c0 + 2])
            acc = acc + gate * nsa[n][:, pair * LANES:(pair + 1) * LANES]
        ob_tiles.append(acc.astype(BF16))
    ob = jnp.concatenate(ob_tiles, axis=1)
    branches = (oa_ref[...], ob, o_c)
    y = jnp.zeros((tm, D_MODEL), F32)
    for n in range(N_BRANCHES):
        mg = jax.nn.sigmoid(_dot(xb, wmg_ref[:, n * D_MODEL:(n + 1) * D_MODEL]))
        y = y + mg * _dot(branches[n], wbr_ref[n])
    mix = _dot(y.astype(BF16), wout_ref[...])
    o_ref[...] = _layer_norm(ALPHA * x + mix, g_ref[...], b_ref[...])


def _merge(x, o_a, o_cmp, o_slc, o_win, o_c, projf, w_mg, w_br, w_out, ln_g, ln_b, B, S,
           *, tm=512):
    T, D = x.shape
    nt = S // tm
    n_slc, n_band = S // SLC_QUERY_CHUNK, S // QUERY_BLOCK
    o_slc = o_slc.reshape(B, n_slc, SLC_QUERY_CHUNK, BRANCH_WIDTH)
    o_win = o_win.reshape(B, n_band, QUERY_BLOCK, BRANCH_WIDTH)
    o_c = o_c.reshape(B, n_band, QUERY_BLOCK, BRANCH_WIDTH)
    row = lambda b, a: (b * nt + a, 0)
    const2 = lambda b, a: (0, 0)
    chunked = lambda b, a: (b, 0, a, 0)
    return pl.pallas_call(
        _merge_kernel,
        grid=(B, nt),
        in_specs=[pl.BlockSpec((tm, D), row),
                  pl.BlockSpec((tm, BRANCH_WIDTH), row),
                  pl.BlockSpec((tm, BRANCH_WIDTH), row),
                  pl.BlockSpec((None, n_slc, tm // n_slc, BRANCH_WIDTH), chunked),
                  pl.BlockSpec((None, n_band, tm // n_band, BRANCH_WIDTH), chunked),
                  pl.BlockSpec((None, n_band, tm // n_band, BRANCH_WIDTH), chunked),
                  pl.BlockSpec((tm, LANES), lambda b, a: (b * nt + a, 2)),
                  pl.BlockSpec((D, N_BRANCHES * D), const2),
                  pl.BlockSpec((N_BRANCHES, BRANCH_WIDTH, D), lambda b, a: (0, 0, 0)),
                  pl.BlockSpec((D, D), const2),
                  pl.BlockSpec((1, D), const2),
                  pl.BlockSpec((1, D), const2)],
        out_specs=pl.BlockSpec((tm, D), row),
        out_shape=jax.ShapeDtypeStruct((T, D), F32),
        compiler_params=_cparams(("parallel", "parallel")),
        name="merge",
    )(x, o_a, o_cmp, o_slc, o_win, o_c, projf, w_mg, w_br, w_out, ln_g, ln_b)


def _dup_kv(w, groups):
    D = w.shape[0]
    w = w.reshape(D, 2 * groups, 1, HEAD_DIM)
    return jnp.broadcast_to(w, (D, 2 * groups, 2, HEAD_DIM)).reshape(D, 4 * groups * HEAD_DIM)


def _split_w_in(w):
    sizes = (512, 512, 512, 512, 256, 256, 256, 24, 512, 128, 128, 3072)
    offs = [0]
    for n in sizes:
        offs.append(offs[-1] + n)
    (a_q, a_k, a_v, b_q, b_kvc, b_kvs, b_kvw, b_g, c_q, c_k, c_v, m_g) = [
        w[:, offs[n]:offs[n + 1]] for n in range(len(sizes))]
    w_bf = jnp.concatenate(
        [a_q, a_k, a_v, b_q, c_q, _dup_kv(b_kvs, 2), _dup_kv(b_kvw, 2),
         _dup_kv(jnp.concatenate([c_k, c_v], axis=1), 2)], axis=1).astype(BF16)
    pad = jnp.zeros((w.shape[0], N_PROJ_F32 - 256 - 24), w.dtype)
    w_f32 = jnp.concatenate([b_kvc, b_g, pad], axis=1).astype(BF16)
    return w_bf, w_f32, m_g.astype(BF16)


def _alibi_slopes(n_heads):
    assert 8 % n_heads == 0
    return jnp.exp2(-8.0 * jnp.arange(1, n_heads + 1, dtype=F32) / n_heads)


def _token_mixing(x, B, S, w_in, diff_lam, diff_subln_g, cmp_pos, cmp_w1, cmp_w2, sinks,
                  w_branch, w_out, ln_g, ln_b, lambda_init):
    G = NSA_KV_GROUPS
    w_bf, w_f32, w_mg = _split_w_in(w_in)
    proj = _proj(x, w_bf, BF16, name="proj_bf16")
    projf = _proj(x, w_f32, F32, name="proj_f32")

    n_rows = S // CMP_STRIDE
    r = projf[:, :4 * HEAD_DIM].reshape(B, S, 2, G, HEAD_DIM).transpose(2, 0, 3, 1, 4)
    r = r.reshape(2, B * G * n_rows, CMP_STRIDE * HEAD_DIM)
    pos = cmp_pos.reshape(2, 2, CMP_STRIDE * HEAD_DIM)
    w2dup = jnp.concatenate([cmp_w2, cmp_w2], axis=-1).astype(BF16)
    kvc = _compress(r, pos, cmp_w1.astype(BF16), w2dup)

    slopes8 = _alibi_slopes(NSA_HEADS)
    o_a = _diff_attn(proj, _alibi_slopes(DIFF_HEADS), diff_lam, diff_subln_g.reshape(1, LANES),
                     B, S, lambda_init, tq=512, tk=512)
    o_cmp, sel = _cmp_attn(proj, kvc, slopes8, B, S)
    o_slc = _gqa_attn(proj, slopes8, B, S, q_blk=BLK_BQ, kv_blk=BLK_SLC, window=None, sel=sel,
                      out_dtype=F32, tq=512, tk=512, name="nsa_slc_attn")
    o_win = _gqa_attn(proj, slopes8, B, S, q_blk=BLK_BQ, kv_blk=BLK_WIN, window=NSA_WINDOW,
                      out_dtype=F32, tq=256, tk=256, name="nsa_win_attn")
    o_c = _gqa_attn(proj, _alibi_slopes(SWA_HEADS), B, S, q_blk=BLK_CQ, kv_blk=BLK_SWA,
                    window=SWA_WINDOW, sinks=sinks, out_dtype=BF16, tq=256, tk=256,
                    name="swa_attn")
    return _merge(x, o_a, o_cmp, o_slc, o_win, o_c, projf, w_mg, w_branch.astype(BF16),
                  w_out.astype(BF16), ln_g, ln_b, B, S)


def kernel(x, p, ffn_w_in, ffn_w_out, ln_g, ln_b, w_in, diff_lam, diff_subln_g, nsa_cmp_pos,
           nsa_cmp_w1, nsa_cmp_w2, swa_sinks, w_branch, w_out, ple_w_in, ple_w_gate):
    B, S, D = x.shape
    T = B * S
    h = x.reshape(T, D)
    for i in range(DEPTH):
        lambda_init = 0.8 - 0.6 * math.exp(-0.3 * i)
        lg = ln_g[i].reshape(3, 1, D)
        lb = ln_b[i].reshape(3, 1, D)
        h = _ffn(h, ffn_w_in[i, 0].astype(BF16), ffn_w_out[i, 0].astype(BF16), lg[0], lb[0])
        h = _token_mixing(h, B, S, w_in[i], diff_lam[i], diff_subln_g[i], nsa_cmp_pos[i],
                          nsa_cmp_w1[i], nsa_cmp_w2[i], swa_sinks[i], w_branch[i], w_out[i],
                          lg[1], lb[1], lambda_init)
        h = _ffn(h, ffn_w_in[i, 1].astype(BF16), ffn_w_out[i, 1].astype(BF16), lg[2], lb[2],
                 ple_args=(p[i].reshape(T, PLE_DIM), ple_w_gate[i].astype(BF16),
                           ple_w_in[i].astype(BF16)))
    return h.reshape(B, S, D)
```

```python
import functools
import math

import jax
import jax.numpy as jnp
from jax import lax
from jax.experimental import pallas as pl
from jax.experimental.pallas import tpu as pltpu

F32 = jnp.float32
BF16 = jnp.bfloat16

D_MODEL = 1024
DEPTH = 2
HEAD_DIM = 64
DIFF_HEADS = 4
NSA_HEADS = 8
NSA_KV_GROUPS = 2
CMP_LEN = 32
CMP_STRIDE = 16
CMP_HIDDEN = 256
SLC_BLOCK = 64
SLC_TOPK = 16
NSA_WINDOW = 512
FORCED_SCORE = 1e4
SWA_HEADS = 8
SWA_KV_HEADS = 2
SWA_WINDOW = 128
QUERY_BLOCK = 128
SLC_QUERY_CHUNK = 32
N_BRANCHES = 3
BRANCH_WIDTH = 512
D_FF = 2816
PLE_DIM = 256
LN_EPS = 1e-5
SUBLN_EPS = 1e-5
ALPHA = (2 * DEPTH) ** 0.25
NEG_INF = -1e30
QK_SCALE = HEAD_DIM ** -0.5

LANES = 128
VMEM_LIMIT = 56 * 1024 * 1024

BLK_AQ, BLK_AK, BLK_AV = 0, 4, 8
BLK_BQ, BLK_CQ = 12, 16
BLK_SLC, BLK_WIN, BLK_SWA = 20, 24, 28
N_PROJ_BF = 32 * LANES
N_PROJ_F32 = 3 * LANES


def _cparams(sem):
    return pltpu.CompilerParams(dimension_semantics=sem, vmem_limit_bytes=VMEM_LIMIT)


def _layer_norm(h, g, b):
    mu = jnp.mean(h, axis=-1, keepdims=True)
    d = h - mu
    var = jnp.mean(d * d, axis=-1, keepdims=True)
    return d * lax.rsqrt(var + LN_EPS) * g + b


def _dot(a, b):
    return jnp.dot(a, b, preferred_element_type=F32)


def _dot_nt(a, b):
    return lax.dot_general(a, b, (((1,), (1,)), ((), ())), preferred_element_type=F32)


EPILOGUE_ROWS = 256


def _ffn_kernel(*refs, n_ff, ple):
    if ple:
        (x_ref, wg_ref, wu_ref, wo_ref, g_ref, b_ref, p_ref, pwg_ref, pwi_ref,
         o_ref, acc_ref, xb_ref) = refs
    else:
        x_ref, wg_ref, wu_ref, wo_ref, g_ref, b_ref, o_ref, acc_ref, xb_ref = refs
    k = pl.program_id(1)

    @pl.when(k == 0)
    def _():
        acc_ref[...] = jnp.zeros_like(acc_ref)
        xb_ref[...] = x_ref[...].astype(BF16)

    xb = xb_ref[...]
    gate = _dot(xb, wg_ref[...])
    up = _dot(xb, wu_ref[...])
    hid = (gate * jax.nn.sigmoid(gate)) * up
    acc_ref[...] += _dot(hid.astype(BF16), wo_ref[...])

    @pl.when(k == n_ff - 1)
    def _():
        for r0 in range(0, x_ref.shape[0], EPILOGUE_ROWS):
            rows = slice(r0, r0 + EPILOGUE_ROWS)
            h = ALPHA * x_ref[rows, :] + 0.5 * acc_ref[rows, :]
            if ple:
                gp = jax.nn.sigmoid(_dot(h.astype(BF16), pwg_ref[...]))
                h = h + gp * _dot(p_ref[rows, :].astype(BF16), pwi_ref[...])
            o_ref[rows, :] = _layer_norm(h, g_ref[...], b_ref[...])


def _ffn(x, w_in, w_out, ln_g, ln_b, ple_args=None, *, tm=1024, tf=256):
    T, D = x.shape
    n_ff = D_FF // tf
    ple = ple_args is not None
    in_specs = [
        pl.BlockSpec((tm, D), lambda i, k: (i, 0)),
        pl.BlockSpec((D, tf), lambda i, k: (0, k)),
        pl.BlockSpec((D, tf), lambda i, k: (0, n_ff + k)),
        pl.BlockSpec((tf, D), lambda i, k: (k, 0)),
        pl.BlockSpec((1, D), lambda i, k: (0, 0)),
        pl.BlockSpec((1, D), lambda i, k: (0, 0)),
    ]
    args = [x, w_in, w_in, w_out, ln_g, ln_b]
    if ple:
        p, pwg, pwi = ple_args
        in_specs += [
            pl.BlockSpec((tm, PLE_DIM), lambda i, k: (i, 0)),
            pl.BlockSpec((D, D), lambda i, k: (0, 0)),
            pl.BlockSpec((PLE_DIM, D), lambda i, k: (0, 0)),
        ]
        args += [p, pwg, pwi]
    return pl.pallas_call(
        functools.partial(_ffn_kernel, n_ff=n_ff, ple=ple),
        grid=(T // tm, n_ff),
        in_specs=in_specs,
        out_specs=pl.BlockSpec((tm, D), lambda i, k: (i, 0)),
        out_shape=jax.ShapeDtypeStruct((T, D), F32),
        scratch_shapes=[pltpu.VMEM((tm, D), F32), pltpu.VMEM((tm, D), BF16)],
        compiler_params=_cparams(("parallel", "arbitrary")),
        name="ffn_ple" if ple else "ffn",
    )(*args)


PROJ_COLS = 1024


def _proj_kernel(x_ref, wb_ref, wf_ref, ob_ref, of_ref):
    xb = x_ref[...].astype(BF16)
    for n0 in range(0, wb_ref.shape[1], PROJ_COLS):
        ob_ref[:, n0:n0 + PROJ_COLS] = _dot(xb, wb_ref[:, n0:n0 + PROJ_COLS]).astype(BF16)
    of_ref[...] = _dot(xb, wf_ref[...])


def _proj(x, w_bf, w_f32, *, tm=512):
    T, K = x.shape
    nb, nf = w_bf.shape[1], w_f32.shape[1]
    return pl.pallas_call(
        _proj_kernel,
        grid=(T // tm,),
        in_specs=[pl.BlockSpec((tm, K), lambda i: (i, 0)),
                  pl.BlockSpec((K, nb), lambda i: (0, 0)),
                  pl.BlockSpec((K, nf), lambda i: (0, 0))],
        out_specs=[pl.BlockSpec((tm, nb), lambda i: (i, 0)),
                   pl.BlockSpec((tm, nf), lambda i: (i, 0))],
        out_shape=[jax.ShapeDtypeStruct((T, nb), BF16), jax.ShapeDtypeStruct((T, nf), F32)],
        compiler_params=_cparams(("parallel",)),
        name="proj",
    )(x, w_bf, w_f32)


def _compress_kernel(r_ref, pos_ref, w1_ref, w2_ref, o_ref):
    r = r_ref[...]
    half = CMP_STRIDE * HEAD_DIM
    lo = _dot((r + pos_ref[0:1, :]).astype(BF16), w1_ref[0:half, :])
    hi = _dot((r + pos_ref[1:2, :]).astype(BF16), w1_ref[half:2 * half, :])
    pre = lo + pltpu.roll(hi, hi.shape[0] - 1, 0)
    hid = jax.nn.gelu(pre, approximate=True)
    o_ref[...] = _dot(hid.astype(BF16), w2_ref[...]).astype(o_ref.dtype)


def _compress(r, pos, w1, w2dup, *, tm=512):
    _, M, F = r.shape
    tm = min(tm, M)
    return pl.pallas_call(
        _compress_kernel,
        grid=(2, M // tm),
        in_specs=[pl.BlockSpec((None, tm, F), lambda n, i: (n, i, 0)),
                  pl.BlockSpec((None, 2, F), lambda n, i: (n, 0, 0)),
                  pl.BlockSpec((None, 2 * F, CMP_HIDDEN), lambda n, i: (n, 0, 0)),
                  pl.BlockSpec((None, CMP_HIDDEN, LANES), lambda n, i: (n, 0, 0))],
        out_specs=pl.BlockSpec((None, tm, LANES), lambda n, i: (n, i, 0)),
        out_shape=jax.ShapeDtypeStruct((2, M, LANES), BF16),
        compiler_params=_cparams(("parallel", "parallel")),
        name="nsa_compress",
    )(r, pos, w1, w2dup)


def _log2(n):
    assert n > 0 and n & (n - 1) == 0, n
    return n.bit_length() - 1


def _stack_heads(q_tiles):
    tq = q_tiles[0].shape[0]
    lane = lax.broadcasted_iota(jnp.int32, (tq, LANES), 1)
    lo = lane < HEAD_DIM
    zero = jnp.zeros((tq, LANES), BF16)
    scaled = [q * jnp.asarray(QK_SCALE, BF16) for q in q_tiles]
    parts = [jnp.where(lo, qs, zero) for qs in scaled] + [jnp.where(lo, zero, qs) for qs in scaled]
    return jnp.concatenate(parts, axis=0)


def _stacked_head_order(nh):
    return list(range(0, nh, 2)) + list(range(1, nh, 2))


def _head_column(values, tq):
    nh = len(values)
    head = lax.broadcasted_iota(jnp.int32, (nh * tq, 1), 0) >> _log2(tq)
    col = jnp.zeros((nh * tq, 1), F32)
    for h in range(nh):
        col = jnp.where(head == h, values[h], col)
    return col


KX_ONE_A, KX_COL_HI, KX_COL_LO, KX_ONE_B = 32, 33, 34, 35
POS_SPLIT = 16


def _key_extras(S, tk):
    assert S // SLC_BLOCK <= KX_ONE_A
    key = jnp.arange(S, dtype=jnp.int32)[:, None]
    lane = jnp.arange(LANES, dtype=jnp.int32)[None, :]
    col = key % tk
    x = jnp.where(lane == key // SLC_BLOCK, -NEG_INF, 0.0)
    x = jnp.where((lane == KX_ONE_A) | (lane == KX_ONE_B), 1.0, x)
    x = jnp.where(lane == KX_COL_HI, (col - col % POS_SPLIT).astype(F32), x)
    x = jnp.where(lane == KX_COL_LO, (col % POS_SPLIT).astype(F32), x)
    return x.astype(BF16)


SOFTMAX_ROWS = 32


def _attn_kernel(*refs, mode, tq, tk, use_sel, nh, lambda_init):
    refs = list(refs)
    slopes_ref = refs.pop(0)
    q_ref, k_ref, v_ref, kx_ref = refs.pop(0), refs.pop(0), refs.pop(0), refs.pop(0)
    sel_ref = refs.pop(0) if use_sel else None
    if mode == "diff":
        lam_ref, subg_ref = refs.pop(0), refs.pop(0)
    o_ref = refs.pop(0)
    mask_ref, qs_ref, s_ref, p_ref, m_ref, acc_ref, shift_ref = refs
    hg = pl.program_id(1)
    i = pl.program_id(2)
    R = nh * tq
    gqa = mode == "gqa"
    ratio = tk // tq
    half = R // 2

    if gqa:
        order = _stacked_head_order(nh)
        qs = _stack_heads([q_ref[:, p * LANES:(p + 1) * LANES] for p in range(nh // 2)])
        slope_col = _head_column([slopes_ref[hg * nh + h] for h in order], tq)
    else:
        qs = _stack_heads([q_ref[...]])
        slope_col = jnp.full((R, 1), slopes_ref[hg], F32)
    shift_ref[...] = slope_col * float(tk)

    @pl.when(i == 0)
    def _():
        row = lax.broadcasted_iota(jnp.int32, (R, tk), 0) & (tq - 1)
        col = lax.broadcasted_iota(jnp.int32, (R, tk), 1)
        for a in range(ratio):
            mask_ref[a] = jnp.where(row - col + a * tq >= 0, 0.0, NEG_INF)

    xl = lax.broadcasted_iota(jnp.int32, (R, LANES), 1)
    xr = lax.broadcasted_iota(jnp.int32, (R, LANES), 0) & (tq - 1)
    xr_lo = xr & (POS_SPLIT - 1)
    qx = jnp.where(xl == KX_ONE_A, -slope_col * (xr - xr_lo).astype(F32), 0.0)
    qx = jnp.where(xl == KX_ONE_B, -slope_col * xr_lo.astype(F32), qx)
    qx = jnp.where((xl == KX_COL_HI) | (xl == KX_COL_LO), slope_col, qx)
    if use_sel:
        n_blk = k_ref.shape[0] // SLC_BLOCK
        not_sel = jnp.concatenate([sel_ref[...].astype(F32) - 1.0] * nh, axis=0)
        qx = jnp.where(xl < n_blk, not_sel, qx)
    qs_ref[:, 0:LANES] = qs
    qs_ref[:, LANES:2 * LANES] = qx.astype(BF16)
    m_ref[...] = jnp.full((R, 1), NEG_INF, F32)
    acc_ref[...] = jnp.zeros(acc_ref.shape, F32)
    lo_half_k = lax.broadcasted_iota(jnp.int32, (tk, LANES), 1) < HEAD_DIM

    def scores(j, slot):
        start = pl.multiple_of(j * tk, tk)
        kk = jnp.concatenate([k_ref[pl.ds(start, tk), :], kx_ref[pl.ds(start, tk), :]], axis=1)
        for h0 in (0, half):
            s_ref[slot, h0:h0 + half, :] = _dot_nt(qs_ref[h0:h0 + half, :], kk)

    def softmax_pv(j, slot, mask_idx):
        start = pl.multiple_of(j * tk, tk)
        v = v_ref[pl.ds(start, tk), :]
        one = jnp.ones((tk, LANES), BF16)
        if gqa:
            v_halves = (jnp.where(lo_half_k, v, one), jnp.where(lo_half_k, one, v))
        else:
            v_halves = (jnp.concatenate([v, one], axis=1),) * 2
        for hh, h0 in enumerate((0, half)):
            for c in range(half // SOFTMAX_ROWS):
                r0 = h0 + c * SOFTMAX_ROWS
                rows = slice(r0, r0 + SOFTMAX_ROWS)
                s = s_ref[slot, rows, :]
                if mask_idx is not None:
                    s = s + mask_ref[mask_idx, rows, :]
                m_prev = m_ref[rows, :] - shift_ref[rows, :]
                m_new = jnp.maximum(m_prev, jnp.max(s, axis=-1, keepdims=True))
                alpha = jnp.exp(m_prev - m_new)
                p_ref[rows, :] = jnp.exp(s - m_new).astype(BF16)
                m_ref[rows, :] = m_new
                acc_ref[rows, :] = alpha * acc_ref[rows, :]
            acc_ref[h0:h0 + half, :] += _dot(p_ref[h0:h0 + half, :], v_halves[hh])

    j_last = i // ratio
    scores(0, 0)

    def pair(t, c):
        scores(2 * t + 1, 1)
        softmax_pv(2 * t, 0, None)
        scores(2 * t + 2, 0)
        softmax_pv(2 * t + 1, 1, None)
        return c

    lax.fori_loop(0, j_last // 2, pair, 0)

    @pl.when(j_last % 2 == 1)
    def _():
        scores(j_last, 1)
        softmax_pv(j_last - 1, 0, None)
        softmax_pv(j_last, 1, i % ratio)

    @pl.when(j_last % 2 == 0)
    def _():
        softmax_pv(j_last, 0, i % ratio)

    acc = acc_ref[...]
    if gqa:
        acc_e, acc_o = acc[:half], acc[half:]
        l_e = acc_e[:, HEAD_DIM:HEAD_DIM + 1]
        l_o = acc_o[:, 0:1]
        o_e = acc_e / l_e
        o_o = acc_o / l_o
        lo_half_q = lax.broadcasted_iota(jnp.int32, (tq, LANES), 1) < HEAD_DIM
        for p_ in range(nh // 2):
            pair = jnp.where(lo_half_q, o_e[p_ * tq:(p_ + 1) * tq, :], o_o[p_ * tq:(p_ + 1) * tq, :])
            o_ref[:, p_ * LANES:(p_ + 1) * LANES] = pair.astype(o_ref.dtype)
    else:
        o = acc[:, 0:LANES] / acc[:, LANES:LANES + 1]
        lam = lam_ref[...]
        lam_full = (jnp.exp(jnp.sum(lam[0:1, :] * lam[1:2, :], axis=-1, keepdims=True))
                    - jnp.exp(jnp.sum(lam[2:3, :] * lam[3:4, :], axis=-1, keepdims=True))
                    + lambda_init)
        od = o[0:tq, :] - lam_full * o[tq:2 * tq, :]
        od = od * lax.rsqrt(jnp.mean(od * od, axis=-1, keepdims=True) + SUBLN_EPS) * subg_ref[...]
        o_ref[...] = (od * (1.0 - lambda_init)).astype(o_ref.dtype)


def _attn_scratch(n_masks, R, tk, acc_width):
    return [pltpu.VMEM((n_masks, R, tk), F32),
            pltpu.VMEM((R, 2 * LANES), BF16),
            pltpu.VMEM((2, R, tk), F32),
            pltpu.VMEM((R, tk), BF16),
            pltpu.VMEM((R, 1), F32),
            pltpu.VMEM((R, acc_width), F32),
            pltpu.VMEM((R, 1), F32)]


def _window_kernel(*refs, tq, back, window, use_sink, nh, n_special):
    refs = list(refs)
    slopes_ref = refs.pop(0)
    sinks_ref = refs.pop(0) if use_sink else None
    q_ref, k_ref, v_ref, o_ref, bias_ref, qs_ref, s_ref, p_ref, m_ref = refs
    g = pl.program_id(1)
    i = pl.program_id(2)
    R = nh * tq
    half = R // 2
    span = back + tq
    order = _stacked_head_order(nh)
    slope_col = _head_column([slopes_ref[g * nh + h] for h in order], tq)

    @pl.when(i == 0)
    def _():
        row = lax.broadcasted_iota(jnp.int32, (R, span), 0) & (tq - 1)
        col = lax.broadcasted_iota(jnp.int32, (R, span), 1)
        for a in range(n_special + 1):
            dist = row - col + (a * tq if a < n_special else back)
            ok = (dist >= 0) & (dist < window)
            bias_ref[a] = jnp.where(ok, -slope_col * dist.astype(F32), NEG_INF)

    qs_ref[...] = _stack_heads([q_ref[:, p * LANES:(p + 1) * LANES] for p in range(nh // 2)])
    a = jnp.minimum(i, n_special)
    start = pl.multiple_of(jnp.maximum(i * tq - back, 0), LANES)
    k = k_ref[pl.ds(start, span), :]
    v = v_ref[pl.ds(start, span), :]
    lo_half_k = lax.broadcasted_iota(jnp.int32, (span, LANES), 1) < HEAD_DIM
    one = jnp.ones((span, LANES), BF16)
    v_halves = (jnp.where(lo_half_k, v, one), jnp.where(lo_half_k, one, v))
    for h0 in (0, half):
        s_ref[h0:h0 + half, :] = _dot_nt(qs_ref[h0:h0 + half, :], k)
    accs = []
    for hh, h0 in enumerate((0, half)):
        for c in range(half // SOFTMAX_ROWS):
            r0 = h0 + c * SOFTMAX_ROWS
            rows = slice(r0, r0 + SOFTMAX_ROWS)
            s = s_ref[rows, :] + bias_ref[a, rows, :]
            m = jnp.max(s, axis=-1, keepdims=True)
            p_ref[rows, :] = jnp.exp(s - m).astype(BF16)
            m_ref[rows, :] = m
        accs.append(_dot(p_ref[h0:h0 + half, :], v_halves[hh]))

    acc_e, acc_o = accs
    l_e = acc_e[:, HEAD_DIM:HEAD_DIM + 1]
    l_o = acc_o[:, 0:1]
    if use_sink:
        m = m_ref[...]
        sink_col = _head_column([sinks_ref[g * nh + h] for h in order], tq)
        m_f = jnp.maximum(m, sink_col)
        shrink = jnp.exp(m - m_f)
        extra = jnp.exp(sink_col - m_f)
        acc_e = acc_e * shrink[:half]
        acc_o = acc_o * shrink[half:]
        l_e = l_e * shrink[:half] + extra[:half]
        l_o = l_o * shrink[half:] + extra[half:]
    o_e = acc_e / l_e
    o_o = acc_o / l_o
    lo_half_q = lax.broadcasted_iota(jnp.int32, (tq, LANES), 1) < HEAD_DIM
    for p_ in range(nh // 2):
        pair = jnp.where(lo_half_q, o_e[p_ * tq:(p_ + 1) * tq, :], o_o[p_ * tq:(p_ + 1) * tq, :])
        o_ref[:, p_ * LANES:(p_ + 1) * LANES] = pair.astype(o_ref.dtype)


def _window_attn(proj, slopes, B, S, *, q_blk, kv_blk, window, sinks=None, out_dtype, tq, name):
    nq = S // tq
    G, nh = 2, 4
    back = -(-window // LANES) * LANES
    n_special = -(-back // tq)
    span = back + tq
    R = nh * tq
    assert span <= S
    kern = functools.partial(_window_kernel, tq=tq, back=back, window=window,
                             use_sink=sinks is not None, nh=nh, n_special=n_special)
    in_specs = [_smem_spec()]
    args = [slopes]
    if sinks is not None:
        in_specs.append(_smem_spec())
        args.append(sinks)
    in_specs += [
        pl.BlockSpec((tq, 2 * LANES), lambda b, g, i: (b * nq + i, q_blk // 2 + g)),
        pl.BlockSpec((S, LANES), lambda b, g, i: (b, kv_blk + g)),
        pl.BlockSpec((S, LANES), lambda b, g, i: (b, kv_blk + 2 + g)),
    ]
    args += [proj, proj, proj]
    return pl.pallas_call(
        kern,
        grid=(B, G, nq),
        in_specs=in_specs,
        out_specs=pl.BlockSpec((tq, 2 * LANES), lambda b, g, i: (b * nq + i, g)),
        out_shape=jax.ShapeDtypeStruct((B * S, BRANCH_WIDTH), out_dtype),
        scratch_shapes=[pltpu.VMEM((n_special + 1, R, span), F32),
                        pltpu.VMEM((R, LANES), BF16),
                        pltpu.VMEM((R, span), F32),
                        pltpu.VMEM((R, span), BF16),
                        pltpu.VMEM((R, 1), F32)],
        compiler_params=_cparams(("parallel", "parallel", "arbitrary")),
        name=name,
    )(*args)


def _smem_spec():
    return pl.BlockSpec(memory_space=pltpu.SMEM)


def _diff_attn(proj, slopes, lam, subg, B, S, lambda_init, *, tq, tk):
    nq = S // tq
    kern = functools.partial(_attn_kernel, mode="diff", tq=tq, tk=tk, use_sel=False, nh=2,
                             lambda_init=lambda_init)
    return pl.pallas_call(
        kern,
        grid=(B, DIFF_HEADS, nq),
        in_specs=[_smem_spec(),
                  pl.BlockSpec((tq, LANES), lambda b, h, i: (b * nq + i, BLK_AQ + h)),
                  pl.BlockSpec((S, LANES), lambda b, h, i: (b, BLK_AK + h)),
                  pl.BlockSpec((S, LANES), lambda b, h, i: (b, BLK_AV + h)),
                  pl.BlockSpec((S, LANES), lambda b, h, i: (0, 0)),
                  pl.BlockSpec((4, HEAD_DIM), lambda b, h, i: (0, 0)),
                  pl.BlockSpec((1, LANES), lambda b, h, i: (0, 0))],
        out_specs=pl.BlockSpec((tq, LANES), lambda b, h, i: (b * nq + i, h)),
        out_shape=jax.ShapeDtypeStruct((B * S, BRANCH_WIDTH), BF16),
        scratch_shapes=_attn_scratch(tk // tq, 2 * tq, tk, 2 * LANES),
        compiler_params=_cparams(("parallel", "parallel", "arbitrary")),
        name="diff_attn",
    )(slopes, proj, proj, proj, _key_extras(S, tk), lam, subg)


def _slc_attn(proj, slopes, sel, B, S, *, tq, tk):
    nq = S // tq
    G, nh = 2, 4
    kern = functools.partial(_attn_kernel, mode="gqa", tq=tq, tk=tk, use_sel=True, nh=nh,
                             lambda_init=0.0)
    in_specs = [
        _smem_spec(),
        pl.BlockSpec((tq, 2 * LANES), lambda b, g, i: (b * nq + i, BLK_BQ // 2 + g)),
        pl.BlockSpec((S, LANES), lambda b, g, i: (b, BLK_SLC + g)),
        pl.BlockSpec((S, LANES), lambda b, g, i: (b, BLK_SLC + 2 + g)),
        pl.BlockSpec((S, LANES), lambda b, g, i: (0, 0)),
        pl.BlockSpec((tq, LANES), lambda b, g, i: (b * nq + i, g)),
    ]
    return pl.pallas_call(
        kern,
        grid=(B, G, nq),
        in_specs=in_specs,
        out_specs=pl.BlockSpec((tq, 2 * LANES), lambda b, g, i: (b * nq + i, g)),
        out_shape=jax.ShapeDtypeStruct((B * S, BRANCH_WIDTH), F32),
        scratch_shapes=_attn_scratch(tk // tq, nh * tq, tk, LANES),
        compiler_params=_cparams(("parallel", "parallel", "arbitrary")),
        name="nsa_slc_attn",
    )(slopes, proj, proj, proj, _key_extras(S, tk), sel)


def _cmp_attn_kernel(slopes_ref, q_ref, kc_ref, vc_ref, o_ref, sel_ref, *, tq, n_cmp, n_blk, n_sel):
    g = pl.program_id(1)
    i = pl.program_id(2)
    nh = NSA_HEADS // NSA_KV_GROUPS
    R = nh * tq
    qs = _stack_heads([q_ref[:, p * LANES:(p + 1) * LANES] for p in range(nh // 2)])
    slope_col = _head_column([slopes_ref[g * nh + h] for h in _stacked_head_order(nh)], tq)

    t = i * tq + (lax.broadcasted_iota(jnp.int32, (R, LANES), 0) & (tq - 1))
    c = lax.broadcasted_iota(jnp.int32, (R, LANES), 1)
    dist = t - (c * CMP_STRIDE + (CMP_LEN - 1))
    valid = (dist >= 0) & (c < n_cmp)
    s = _dot_nt(qs, kc_ref[...])
    s = jnp.where(valid, s - slope_col * dist.astype(F32), NEG_INF)
    m = jnp.max(s, axis=-1, keepdims=True)
    e = jnp.exp(s - m)
    p_c = e / jnp.sum(e, axis=-1, keepdims=True)
    any_valid = (t[:, 0:1] >= CMP_LEN - 1).astype(F32)
    p_c = p_c * any_valid
    o = _dot(p_c.astype(BF16), vc_ref[...])
    lane = lax.broadcasted_iota(jnp.int32, (tq, LANES), 1)
    lo_half = lane < HEAD_DIM
    for p in range(nh // 2):
        pair = jnp.where(lo_half, o[p * tq:(p + 1) * tq, :],
                         o[(nh // 2 + p) * tq:(nh // 2 + p + 1) * tq, :])
        o_ref[:, p * LANES:(p + 1) * LANES] = pair.astype(o_ref.dtype)

    p_sum = p_c[0:tq, :]
    for h in range(1, nh):
        p_sum = p_sum + p_c[h * tq:(h + 1) * tq, :]
    jj = lax.broadcasted_iota(jnp.int32, (n_blk, LANES), 0)
    cc = lax.broadcasted_iota(jnp.int32, (n_blk, LANES), 1)
    overlap = (jnp.minimum(cc * CMP_STRIDE + CMP_LEN, jj * SLC_BLOCK + SLC_BLOCK)
               - jnp.maximum(cc * CMP_STRIDE, jj * SLC_BLOCK))
    overlap = jnp.where(cc < n_cmp, jnp.maximum(overlap, 0), 0)
    w_t = (overlap.astype(F32) * (1.0 / CMP_LEN)).astype(BF16)
    hi = p_sum.astype(BF16)
    rem = p_sum - hi.astype(F32)
    mid = rem.astype(BF16)
    low = (rem - mid.astype(F32)).astype(BF16)
    score = _dot_nt(w_t, hi) + _dot_nt(w_t, mid) + _dot_nt(w_t, low)

    tpos = i * tq + lax.broadcasted_iota(jnp.int32, (n_blk, tq), 1)
    t_blk = tpos >> _log2(SLC_BLOCK)
    jb = lax.broadcasted_iota(jnp.int32, (n_blk, tq), 0)
    score = jnp.where((jb == 0) | (jb == t_blk) | (jb == t_blk - 1), FORCED_SCORE, score)
    score = jnp.where(jb > t_blk, -1.0, score)
    rank = jnp.zeros((n_blk, tq), F32)
    for kk in range(n_blk):
        row = score[kk:kk + 1, :]
        ahead = (row > score) | ((row == score) & (kk < jb))
        rank = rank + ahead.astype(F32)
    sel_t = (rank < n_sel).astype(BF16)
    sel_t = jnp.concatenate([sel_t, jnp.zeros((LANES - n_blk, tq), BF16)], axis=0)
    eye = (lax.broadcasted_iota(jnp.int32, (tq, tq), 0)
           == lax.broadcasted_iota(jnp.int32, (tq, tq), 1)).astype(BF16)
    sel_ref[...] = _dot_nt(eye, sel_t).astype(sel_ref.dtype)


def _cmp_attn(proj, kvc, slopes, B, S, *, tq=256):
    nq = S // tq
    G = NSA_KV_GROUPS
    n_cmp = (S - CMP_LEN) // CMP_STRIDE + 1
    n_blk = S // SLC_BLOCK
    n_sel = min(SLC_TOPK, n_blk)
    kern = functools.partial(_cmp_attn_kernel, tq=tq, n_cmp=n_cmp, n_blk=n_blk, n_sel=n_sel)
    return pl.pallas_call(
        kern,
        grid=(B, G, nq),
        in_specs=[_smem_spec(),
                  pl.BlockSpec((tq, 2 * LANES), lambda b, g, i: (b * nq + i, BLK_BQ // 2 + g)),
                  pl.BlockSpec((None, LANES, LANES), lambda b, g, i: (0, b * G + g, 0)),
                  pl.BlockSpec((None, LANES, LANES), lambda b, g, i: (1, b * G + g, 0))],
        out_specs=[pl.BlockSpec((tq, 2 * LANES), lambda b, g, i: (b * nq + i, g)),
                   pl.BlockSpec((tq, LANES), lambda b, g, i: (b * nq + i, g))],
        out_shape=[jax.ShapeDtypeStruct((B * S, BRANCH_WIDTH), F32),
                   jax.ShapeDtypeStruct((B * S, G * LANES), BF16)],
        compiler_params=_cparams(("parallel", "parallel", "parallel")),
        name="nsa_cmp_attn",
    )(slopes, proj, kvc, kvc)


def _interleave_matrix(tm, n_outer):
    e_cnt = tm // n_outer
    r_out = lax.broadcasted_iota(jnp.int32, (tm, tm), 0)
    r_in = lax.broadcasted_iota(jnp.int32, (tm, tm), 1)
    outer = r_in >> _log2(e_cnt)
    inner = r_in & (e_cnt - 1)
    return (r_out == inner * n_outer + outer).astype(BF16)


def _permute_rows(pm, x):
    if x.dtype == BF16:
        return _dot(pm, x)
    hi = x.astype(BF16)
    rem = x - hi.astype(F32)
    mid = rem.astype(BF16)
    low = (rem - mid.astype(F32)).astype(BF16)
    return _dot(pm, hi) + _dot(pm, mid) + _dot(pm, low)


def _merge_kernel(x_ref, oa_ref, ocmp_ref, oslc_ref, owin_ref, oc_ref, gts_ref,
                  wmg_ref, wbr_ref, wout_ref, g_ref, b_ref, o_ref):
    tm = x_ref.shape[0]
    x = x_ref[...]
    xb = x.astype(BF16)
    p_slc = _interleave_matrix(tm, oslc_ref.shape[0])
    p_band = _interleave_matrix(tm, owin_ref.shape[0])
    o_slc = _permute_rows(p_slc, oslc_ref[...].reshape(tm, BRANCH_WIDTH))
    o_win = _permute_rows(p_band, owin_ref[...].reshape(tm, BRANCH_WIDTH))
    o_c = _permute_rows(p_band, oc_ref[...].reshape(tm, BRANCH_WIDTH)).astype(BF16)
    gates = jax.nn.sigmoid(gts_ref[...])
    lane = lax.broadcasted_iota(jnp.int32, (tm, LANES), 1)
    lo_half = lane < HEAD_DIM
    nsa = (ocmp_ref[...], o_slc, o_win)
    ob_tiles = []
    for pair in range(NSA_HEADS // 2):
        acc = jnp.zeros((tm, LANES), F32)
        for n in range(3):
            c0 = n * NSA_HEADS + 2 * pair
            gate = jnp.where(lo_half, gates[:, c0:c0 + 1], gates[:, c0 + 1:c0 + 2])
            acc = acc + gate * nsa[n][:, pair * LANES:(pair + 1) * LANES]
        ob_tiles.append(acc.astype(BF16))
    ob = jnp.concatenate(ob_tiles, axis=1)
    branches = (oa_ref[...], ob, o_c)
    y = jnp.zeros((tm, D_MODEL), F32)
    for n in range(N_BRANCHES):
        mg = jax.nn.sigmoid(_dot(xb, wmg_ref[:, n * D_MODEL:(n + 1) * D_MODEL]))
        y = y + mg * _dot(branches[n], wbr_ref[n])
    mix = _dot(y.astype(BF16), wout_ref[...])
    o_ref[...] = _layer_norm(ALPHA * x + mix, g_ref[...], b_ref[...])


def _merge(x, o_a, o_cmp, o_slc, o_win, o_c, projf, w_mg, w_br, w_out, ln_g, ln_b, B, S,
           *, tm=512):
    T, D = x.shape
    nt = S // tm
    n_slc, n_band = S // SLC_QUERY_CHUNK, S // QUERY_BLOCK
    o_slc = o_slc.reshape(B, n_slc, SLC_QUERY_CHUNK, BRANCH_WIDTH)
    o_win = o_win.reshape(B, n_band, QUERY_BLOCK, BRANCH_WIDTH)
    o_c = o_c.reshape(B, n_band, QUERY_BLOCK, BRANCH_WIDTH)
    row = lambda b, a: (b * nt + a, 0)
    const2 = lambda b, a: (0, 0)
    chunked = lambda b, a: (b, 0, a, 0)
    return pl.pallas_call(
        _merge_kernel,
        grid=(B, nt),
        in_specs=[pl.BlockSpec((tm, D), row),
                  pl.BlockSpec((tm, BRANCH_WIDTH), row),
                  pl.BlockSpec((tm, BRANCH_WIDTH), row),
                  pl.BlockSpec((None, n_slc, tm // n_slc, BRANCH_WIDTH), chunked),
                  pl.BlockSpec((None, n_band, tm // n_band, BRANCH_WIDTH), chunked),
                  pl.BlockSpec((None, n_band, tm // n_band, BRANCH_WIDTH), chunked),
                  pl.BlockSpec((tm, LANES), lambda b, a: (b * nt + a, 2)),
                  pl.BlockSpec((D, N_BRANCHES * D), const2),
                  pl.BlockSpec((N_BRANCHES, BRANCH_WIDTH, D), lambda b, a: (0, 0, 0)),
                  pl.BlockSpec((D, D), const2),
                  pl.BlockSpec((1, D), const2),
                  pl.BlockSpec((1, D), const2)],
        out_specs=pl.BlockSpec((tm, D), row),
        out_shape=jax.ShapeDtypeStruct((T, D), F32),
        compiler_params=_cparams(("parallel", "parallel")),
        name="merge",
    )(x, o_a, o_cmp, o_slc, o_win, o_c, projf, w_mg, w_br, w_out, ln_g, ln_b)


def _dup_kv(w, groups):
    D = w.shape[0]
    w = w.reshape(D, 2 * groups, 1, HEAD_DIM)
    return jnp.broadcast_to(w, (D, 2 * groups, 2, HEAD_DIM)).reshape(D, 4 * groups * HEAD_DIM)


def _split_w_in(w):
    sizes = (512, 512, 512, 512, 256, 256, 256, 24, 512, 128, 128, 3072)
    offs = [0]
    for n in sizes:
        offs.append(offs[-1] + n)
    (a_q, a_k, a_v, b_q, b_kvc, b_kvs, b_kvw, b_g, c_q, c_k, c_v, m_g) = [
        w[:, offs[n]:offs[n + 1]] for n in range(len(sizes))]
    w_bf = jnp.concatenate(
        [a_q, a_k, a_v, b_q, c_q, _dup_kv(b_kvs, 2), _dup_kv(b_kvw, 2),
         _dup_kv(jnp.concatenate([c_k, c_v], axis=1), 2)], axis=1).astype(BF16)
    pad = jnp.zeros((w.shape[0], N_PROJ_F32 - 256 - 24), w.dtype)
    w_f32 = jnp.concatenate([b_kvc, b_g, pad], axis=1).astype(BF16)
    return w_bf, w_f32, m_g.astype(BF16)


def _alibi_slopes(n_heads):
    assert 8 % n_heads == 0
    return jnp.exp2(-8.0 * jnp.arange(1, n_heads + 1, dtype=F32) / n_heads)


def _token_mixing(x, B, S, w_in, diff_lam, diff_subln_g, cmp_pos, cmp_w1, cmp_w2, sinks,
                  w_branch, w_out, ln_g, ln_b, lambda_init):
    G = NSA_KV_GROUPS
    w_bf, w_f32, w_mg = _split_w_in(w_in)
    proj, projf = _proj(x, w_bf, w_f32)

    n_rows = S // CMP_STRIDE
    r = projf[:, :4 * HEAD_DIM].reshape(B, S, 2, G, HEAD_DIM).transpose(2, 0, 3, 1, 4)
    r = r.reshape(2, B * G * n_rows, CMP_STRIDE * HEAD_DIM)
    pos = cmp_pos.reshape(2, 2, CMP_STRIDE * HEAD_DIM)
    w2dup = jnp.concatenate([cmp_w2, cmp_w2], axis=-1).astype(BF16)
    kvc = _compress(r, pos, cmp_w1.astype(BF16), w2dup)

    slopes8 = _alibi_slopes(NSA_HEADS)
    o_a = _diff_attn(proj, _alibi_slopes(DIFF_HEADS), diff_lam, diff_subln_g.reshape(1, LANES),
                     B, S, lambda_init, tq=512, tk=512)
    o_cmp, sel = _cmp_attn(proj, kvc, slopes8, B, S)
    o_slc = _slc_attn(proj, slopes8, sel, B, S, tq=512, tk=512)
    o_win = _window_attn(proj, slopes8, B, S, q_blk=BLK_BQ, kv_blk=BLK_WIN, window=NSA_WINDOW,
                         out_dtype=F32, tq=256, name="nsa_win_attn")
    o_c = _window_attn(proj, _alibi_slopes(SWA_HEADS), B, S, q_blk=BLK_CQ, kv_blk=BLK_SWA,
                       window=SWA_WINDOW, sinks=sinks, out_dtype=BF16, tq=256, name="swa_attn")
    return _merge(x, o_a, o_cmp, o_slc, o_win, o_c, projf, w_mg, w_branch.astype(BF16),
                  w_out.astype(BF16), ln_g, ln_b, B, S)


def kernel(x, p, ffn_w_in, ffn_w_out, ln_g, ln_b, w_in, diff_lam, diff_subln_g, nsa_cmp_pos,
           nsa_cmp_w1, nsa_cmp_w2, swa_sinks, w_branch, w_out, ple_w_in, ple_w_gate):
    B, S, D = x.shape
    T = B * S
    h = x.reshape(T, D)
    for i in range(DEPTH):
        lambda_init = 0.8 - 0.6 * math.exp(-0.3 * i)
        lg = ln_g[i].reshape(3, 1, D)
        lb = ln_b[i].reshape(3, 1, D)
        h = _ffn(h, ffn_w_in[i, 0].astype(BF16), ffn_w_out[i, 0].astype(BF16), lg[0], lb[0])
        h = _token_mixing(h, B, S, w_in[i], diff_lam[i], diff_subln_g[i], nsa_cmp_pos[i],
                          nsa_cmp_w1[i], nsa_cmp_w2[i], swa_sinks[i], w_branch[i], w_out[i],
                          lg[1], lb[1], lambda_init)
        h = _ffn(h, ffn_w_in[i, 1].astype(BF16), ffn_w_out[i, 1].astype(BF16), lg[2], lb[2],
                 ple_args=(p[i].reshape(T, PLE_DIM), ple_w_gate[i].astype(BF16),
                           ple_w_in[i].astype(BF16)))
    return h.reshape(B, S, D)
```

```python
import functools
import math

import jax
import jax.numpy as jnp
from jax import lax
from jax.experimental import pallas as pl
from jax.experimental.pallas import tpu as pltpu

F32 = jnp.float32
BF16 = jnp.bfloat16

D_MODEL = 1024
DEPTH = 2
HEAD_DIM = 64
DIFF_HEADS = 4
NSA_HEADS = 8
NSA_KV_GROUPS = 2
CMP_LEN = 32
CMP_STRIDE = 16
CMP_HIDDEN = 256
SLC_BLOCK = 64
SLC_TOPK = 16
NSA_WINDOW = 512
FORCED_SCORE = 1e4
SWA_HEADS = 8
SWA_KV_HEADS = 2
SWA_WINDOW = 128
QUERY_BLOCK = 128
SLC_QUERY_CHUNK = 32
N_BRANCHES = 3
BRANCH_WIDTH = 512
D_FF = 2816
PLE_DIM = 256
LN_EPS = 1e-5
SUBLN_EPS = 1e-5
ALPHA = (2 * DEPTH) ** 0.25
NEG_INF = -1e30
QK_SCALE = HEAD_DIM ** -0.5

LANES = 128
VMEM_LIMIT = 56 * 1024 * 1024

BLK_AQ, BLK_AK, BLK_AV = 0, 4, 8
BLK_BQ, BLK_CQ = 12, 16
BLK_SLC, BLK_WIN, BLK_SWA = 20, 24, 28
N_PROJ_BF = 32 * LANES
N_PROJ_F32 = 3 * LANES


def _cparams(sem):
    return pltpu.CompilerParams(dimension_semantics=sem, vmem_limit_bytes=VMEM_LIMIT)


def _layer_norm(h, g, b):
    mu = jnp.mean(h, axis=-1, keepdims=True)
    d = h - mu
    var = jnp.mean(d * d, axis=-1, keepdims=True)
    return d * lax.rsqrt(var + LN_EPS) * g + b


def _dot(a, b):
    return jnp.dot(a, b, preferred_element_type=F32)


def _dot_nt(a, b):
    return lax.dot_general(a, b, (((1,), (1,)), ((), ())), preferred_element_type=F32)


FF_CHUNK = 256
EPILOGUE_ROWS = 256


def _ffn_kernel(*refs, ple):
    if ple:
        x_ref, win_ref, wout_ref, g_ref, b_ref, p_ref, pwg_ref, pwi_ref, o_ref, acc_ref = refs
    else:
        x_ref, win_ref, wout_ref, g_ref, b_ref, o_ref, acc_ref = refs
    xb = x_ref[...].astype(BF16)
    for k in range(D_FF // FF_CHUNK):
        c0 = k * FF_CHUNK
        gate = _dot(xb, win_ref[:, c0:c0 + FF_CHUNK])
        up = _dot(xb, win_ref[:, D_FF + c0:D_FF + c0 + FF_CHUNK])
        hid = ((gate * jax.nn.sigmoid(gate)) * up).astype(BF16)
        contrib = _dot(hid, wout_ref[c0:c0 + FF_CHUNK, :])
        if k == 0:
            acc_ref[...] = contrib
        else:
            acc_ref[...] += contrib
    for r0 in range(0, x_ref.shape[0], EPILOGUE_ROWS):
        rows = slice(r0, r0 + EPILOGUE_ROWS)
        h = ALPHA * x_ref[rows, :] + 0.5 * acc_ref[rows, :]
        if ple:
            gp = jax.nn.sigmoid(_dot(h.astype(BF16), pwg_ref[...]))
            h = h + gp * _dot(p_ref[rows, :].astype(BF16), pwi_ref[...])
        o_ref[rows, :] = _layer_norm(h, g_ref[...], b_ref[...])


def _resident(shape):
    return pl.BlockSpec(shape, lambda i: (0,) * len(shape), pipeline_mode=pl.Buffered(1))


def _ffn(x, w_in, w_out, ln_g, ln_b, ple_args=None, *, tm=512):
    T, D = x.shape
    ple = ple_args is not None
    in_specs = [pl.BlockSpec((tm, D), lambda i: (i, 0)),
                _resident(w_in.shape), _resident(w_out.shape),
                _resident((1, D)), _resident((1, D))]
    args = [x, w_in, w_out, ln_g, ln_b]
    if ple:
        p, pwg, pwi = ple_args
        in_specs += [pl.BlockSpec((tm, PLE_DIM), lambda i: (i, 0)),
                     _resident(pwg.shape), _resident(pwi.shape)]
        args += [p, pwg, pwi]
    return pl.pallas_call(
        functools.partial(_ffn_kernel, ple=ple),
        grid=(T // tm,),
        in_specs=in_specs,
        out_specs=pl.BlockSpec((tm, D), lambda i: (i, 0)),
        out_shape=jax.ShapeDtypeStruct((T, D), F32),
        scratch_shapes=[pltpu.VMEM((tm, D), F32)],
        compiler_params=_cparams(("parallel",)),
        name="ffn_ple" if ple else "ffn",
    )(*args)


PROJ_COLS = 1024


def _proj_kernel(x_ref, wb_ref, wf_ref, ob_ref, of_ref):
    xb = x_ref[...].astype(BF16)
    for n0 in range(0, wb_ref.shape[1], PROJ_COLS):
        ob_ref[:, n0:n0 + PROJ_COLS] = _dot(xb, wb_ref[:, n0:n0 + PROJ_COLS]).astype(BF16)
    of_ref[...] = _dot(xb, wf_ref[...])


def _proj(x, w_bf, w_f32, *, tm=512):
    T, K = x.shape
    nb, nf = w_bf.shape[1], w_f32.shape[1]
    return pl.pallas_call(
        _proj_kernel,
        grid=(T // tm,),
        in_specs=[pl.BlockSpec((tm, K), lambda i: (i, 0)),
                  pl.BlockSpec((K, nb), lambda i: (0, 0)),
                  pl.BlockSpec((K, nf), lambda i: (0, 0))],
        out_specs=[pl.BlockSpec((tm, nb), lambda i: (i, 0)),
                   pl.BlockSpec((tm, nf), lambda i: (i, 0))],
        out_shape=[jax.ShapeDtypeStruct((T, nb), BF16), jax.ShapeDtypeStruct((T, nf), F32)],
        compiler_params=_cparams(("parallel",)),
        name="proj",
    )(x, w_bf, w_f32)


def _compress_kernel(r_ref, pos_ref, w1_ref, w2_ref, o_ref):
    r = r_ref[...]
    half = CMP_STRIDE * HEAD_DIM
    lo = _dot((r + pos_ref[0:1, :]).astype(BF16), w1_ref[0:half, :])
    hi = _dot((r + pos_ref[1:2, :]).astype(BF16), w1_ref[half:2 * half, :])
    pre = lo + pltpu.roll(hi, hi.shape[0] - 1, 0)
    hid = jax.nn.gelu(pre, approximate=True)
    o_ref[...] = _dot(hid.astype(BF16), w2_ref[...]).astype(o_ref.dtype)


def _compress(r, pos, w1, w2dup, *, tm=512):
    _, M, F = r.shape
    tm = min(tm, M)
    return pl.pallas_call(
        _compress_kernel,
        grid=(2, M // tm),
        in_specs=[pl.BlockSpec((None, tm, F), lambda n, i: (n, i, 0)),
                  pl.BlockSpec((None, 2, F), lambda n, i: (n, 0, 0)),
                  pl.BlockSpec((None, 2 * F, CMP_HIDDEN), lambda n, i: (n, 0, 0)),
                  pl.BlockSpec((None, CMP_HIDDEN, LANES), lambda n, i: (n, 0, 0))],
        out_specs=pl.BlockSpec((None, tm, LANES), lambda n, i: (n, i, 0)),
        out_shape=jax.ShapeDtypeStruct((2, M, LANES), BF16),
        compiler_params=_cparams(("parallel", "parallel")),
        name="nsa_compress",
    )(r, pos, w1, w2dup)


def _log2(n):
    assert n > 0 and n & (n - 1) == 0, n
    return n.bit_length() - 1


def _stack_heads(q_tiles):
    tq = q_tiles[0].shape[0]
    lane = lax.broadcasted_iota(jnp.int32, (tq, LANES), 1)
    lo = lane < HEAD_DIM
    zero = jnp.zeros((tq, LANES), BF16)
    scaled = [q * jnp.asarray(QK_SCALE, BF16) for q in q_tiles]
    parts = [jnp.where(lo, qs, zero) for qs in scaled] + [jnp.where(lo, zero, qs) for qs in scaled]
    return jnp.concatenate(parts, axis=0)


def _stacked_head_order(nh):
    return list(range(0, nh, 2)) + list(range(1, nh, 2))


def _head_column(values, tq):
    nh = len(values)
    head = lax.broadcasted_iota(jnp.int32, (nh * tq, 1), 0) >> _log2(tq)
    col = jnp.zeros((nh * tq, 1), F32)
    for h in range(nh):
        col = jnp.where(head == h, values[h], col)
    return col


KX_ONE_A, KX_COL_HI, KX_COL_LO, KX_ONE_B = 32, 33, 34, 35
POS_SPLIT = 16


def _key_extras(S, tk):
    assert S // SLC_BLOCK <= KX_ONE_A
    key = jnp.arange(S, dtype=jnp.int32)[:, None]
    lane = jnp.arange(LANES, dtype=jnp.int32)[None, :]
    col = key % tk
    x = jnp.where(lane == key // SLC_BLOCK, -NEG_INF, 0.0)
    x = jnp.where((lane == KX_ONE_A) | (lane == KX_ONE_B), 1.0, x)
    x = jnp.where(lane == KX_COL_HI, (col - col % POS_SPLIT).astype(F32), x)
    x = jnp.where(lane == KX_COL_LO, (col % POS_SPLIT).astype(F32), x)
    return x.astype(BF16)


SOFTMAX_ROWS = 32


def _attn_kernel(*refs, mode, tq, tk, use_sel, nh, lambda_init):
    refs = list(refs)
    slopes_ref = refs.pop(0)
    q_ref, k_ref, v_ref, kx_ref = refs.pop(0), refs.pop(0), refs.pop(0), refs.pop(0)
    sel_ref = refs.pop(0) if use_sel else None
    if mode == "diff":
        lam_ref, subg_ref = refs.pop(0), refs.pop(0)
    o_ref = refs.pop(0)
    mask_ref, qs_ref, s_ref, p_ref, m_ref, acc_ref, shift_ref = refs
    hg = pl.program_id(1)
    i = pl.program_id(2)
    R = nh * tq
    gqa = mode == "gqa"
    ratio = tk // tq
    half = R // 2

    xl = lax.broadcasted_iota(jnp.int32, (R, LANES), 1)

    @pl.when(i == 0)
    def _():
        if gqa:
            slope_col = _head_column([slopes_ref[hg * nh + h] for h in _stacked_head_order(nh)], tq)
        else:
            slope_col = jnp.full((R, 1), slopes_ref[hg], F32)
        shift_ref[...] = slope_col * float(tk)
        row = lax.broadcasted_iota(jnp.int32, (R, tk), 0) & (tq - 1)
        col = lax.broadcasted_iota(jnp.int32, (R, tk), 1)
        for a in range(ratio):
            mask_ref[a] = jnp.where(row - col + a * tq >= 0, 0.0, NEG_INF)
        xr = lax.broadcasted_iota(jnp.int32, (R, LANES), 0) & (tq - 1)
        xr_lo = xr & (POS_SPLIT - 1)
        qx = jnp.where(xl == KX_ONE_A, -slope_col * (xr - xr_lo).astype(F32), 0.0)
        qx = jnp.where(xl == KX_ONE_B, -slope_col * xr_lo.astype(F32), qx)
        qx = jnp.where((xl == KX_COL_HI) | (xl == KX_COL_LO), slope_col, qx)
        qs_ref[:, LANES:2 * LANES] = qx.astype(BF16)

    if gqa:
        qs_ref[:, 0:LANES] = _stack_heads([q_ref[:, p * LANES:(p + 1) * LANES]
                                           for p in range(nh // 2)])
    else:
        qs_ref[:, 0:LANES] = _stack_heads([q_ref[...]])
    if use_sel:
        n_blk = k_ref.shape[0] // SLC_BLOCK
        not_sel = jnp.concatenate([(sel_ref[...].astype(F32) - 1.0).astype(BF16)] * nh, axis=0)
        qs_ref[:, LANES:2 * LANES] = jnp.where(xl < n_blk, not_sel, qs_ref[:, LANES:2 * LANES])
    m_ref[...] = jnp.full((R, 1), NEG_INF, F32)
    acc_ref[...] = jnp.zeros(acc_ref.shape, F32)
    lo_half_k = lax.broadcasted_iota(jnp.int32, (tk, LANES), 1) < HEAD_DIM

    def scores(j, slot):
        start = pl.multiple_of(j * tk, tk)
        kk = jnp.concatenate([k_ref[pl.ds(start, tk), :], kx_ref[pl.ds(start, tk), :]], axis=1)
        for h0 in (0, half):
            s_ref[slot, h0:h0 + half, :] = _dot_nt(qs_ref[h0:h0 + half, :], kk)

    def softmax_pv(j, slot, mask_idx):
        start = pl.multiple_of(j * tk, tk)
        v = v_ref[pl.ds(start, tk), :]
        one = jnp.ones((tk, LANES), BF16)
        if gqa:
            v_halves = (jnp.where(lo_half_k, v, one), jnp.where(lo_half_k, one, v))
        else:
            v_halves = (jnp.concatenate([v, one], axis=1),) * 2
        for hh, h0 in enumerate((0, half)):
            for c in range(half // SOFTMAX_ROWS):
                r0 = h0 + c * SOFTMAX_ROWS
                rows = slice(r0, r0 + SOFTMAX_ROWS)
                s = s_ref[slot, rows, :]
                if mask_idx is not None:
                    s = s + mask_ref[mask_idx, rows, :]
                m_prev = m_ref[rows, :] - shift_ref[rows, :]
                m_new = jnp.maximum(m_prev, jnp.max(s, axis=-1, keepdims=True))
                alpha = jnp.exp(m_prev - m_new)
                p_ref[rows, :] = jnp.exp(s - m_new).astype(BF16)
                m_ref[rows, :] = m_new
                acc_ref[rows, :] = alpha * acc_ref[rows, :]
            acc_ref[h0:h0 + half, :] += _dot(p_ref[h0:h0 + half, :], v_halves[hh])

    j_last = i // ratio
    scores(0, 0)

    def pair(t, c):
        scores(2 * t + 1, 1)
        softmax_pv(2 * t, 0, None)
        scores(2 * t + 2, 0)
        softmax_pv(2 * t + 1, 1, None)
        return c

    lax.fori_loop(0, j_last // 2, pair, 0)

    @pl.when(j_last % 2 == 1)
    def _():
        scores(j_last, 1)
        softmax_pv(j_last - 1, 0, None)
        softmax_pv(j_last, 1, i % ratio)

    @pl.when(j_last % 2 == 0)
    def _():
        softmax_pv(j_last, 0, i % ratio)

    acc = acc_ref[...]
    if gqa:
        acc_e, acc_o = acc[:half], acc[half:]
        l_e = acc_e[:, HEAD_DIM:HEAD_DIM + 1]
        l_o = acc_o[:, 0:1]
        o_e = acc_e / l_e
        o_o = acc_o / l_o
        lo_half_q = lax.broadcasted_iota(jnp.int32, (tq, LANES), 1) < HEAD_DIM
        for p_ in range(nh // 2):
            pair = jnp.where(lo_half_q, o_e[p_ * tq:(p_ + 1) * tq, :], o_o[p_ * tq:(p_ + 1) * tq, :])
            o_ref[:, p_ * LANES:(p_ + 1) * LANES] = pair.astype(o_ref.dtype)
    else:
        o = acc[:, 0:LANES] / acc[:, LANES:LANES + 1]
        lam = lam_ref[...]
        lam_full = (jnp.exp(jnp.sum(lam[0:1, :] * lam[1:2, :], axis=-1, keepdims=True))
                    - jnp.exp(jnp.sum(lam[2:3, :] * lam[3:4, :], axis=-1, keepdims=True))
                    + lambda_init)
        od = o[0:tq, :] - lam_full * o[tq:2 * tq, :]
        od = od * lax.rsqrt(jnp.mean(od * od, axis=-1, keepdims=True) + SUBLN_EPS) * subg_ref[...]
        o_ref[...] = (od * (1.0 - lambda_init)).astype(o_ref.dtype)


def _attn_scratch(n_masks, R, tk, acc_width):
    return [pltpu.VMEM((n_masks, R, tk), F32),
            pltpu.VMEM((R, 2 * LANES), BF16),
            pltpu.VMEM((2, R, tk), F32),
            pltpu.VMEM((R, tk), BF16),
            pltpu.VMEM((R, 1), F32),
            pltpu.VMEM((R, acc_width), F32),
            pltpu.VMEM((R, 1), F32)]


def _window_kernel(*refs, tq, back, window, use_sink, nh, n_special):
    refs = list(refs)
    slopes_ref = refs.pop(0)
    sinks_ref = refs.pop(0) if use_sink else None
    q_ref, k_ref, v_ref, o_ref, bias_ref, qs_ref, s_ref, p_ref, m_ref, sink_ref = refs
    g = pl.program_id(1)
    i = pl.program_id(2)
    R = nh * tq
    half = R // 2
    span = back + tq
    order = _stacked_head_order(nh)

    @pl.when(i == 0)
    def _():
        slope_col = _head_column([slopes_ref[g * nh + h] for h in order], tq)
        if use_sink:
            sink_ref[...] = _head_column([sinks_ref[g * nh + h] for h in order], tq)
        row = lax.broadcasted_iota(jnp.int32, (R, span), 0) & (tq - 1)
        col = lax.broadcasted_iota(jnp.int32, (R, span), 1)
        for a in range(n_special + 1):
            dist = row - col + (a * tq if a < n_special else back)
            ok = (dist >= 0) & (dist < window)
            bias_ref[a] = jnp.where(ok, -slope_col * dist.astype(F32), NEG_INF)

    qs_ref[...] = _stack_heads([q_ref[:, p * LANES:(p + 1) * LANES] for p in range(nh // 2)])
    a = jnp.minimum(i, n_special)
    start = pl.multiple_of(jnp.maximum(i * tq - back, 0), LANES)
    k = k_ref[pl.ds(start, span), :]
    v = v_ref[pl.ds(start, span), :]
    lo_half_k = lax.broadcasted_iota(jnp.int32, (span, LANES), 1) < HEAD_DIM
    one = jnp.ones((span, LANES), BF16)
    v_halves = (jnp.where(lo_half_k, v, one), jnp.where(lo_half_k, one, v))
    for h0 in (0, half):
        s_ref[h0:h0 + half, :] = _dot_nt(qs_ref[h0:h0 + half, :], k)
    accs = []
    for hh, h0 in enumerate((0, half)):
        for c in range(half // SOFTMAX_ROWS):
            r0 = h0 + c * SOFTMAX_ROWS
            rows = slice(r0, r0 + SOFTMAX_ROWS)
            s = s_ref[rows, :] + bias_ref[a, rows, :]
            m = jnp.max(s, axis=-1, keepdims=True)
            p_ref[rows, :] = jnp.exp(s - m).astype(BF16)
            m_ref[rows, :] = m
        accs.append(_dot(p_ref[h0:h0 + half, :], v_halves[hh]))

    acc_e, acc_o = accs
    l_e = acc_e[:, HEAD_DIM:HEAD_DIM + 1]
    l_o = acc_o[:, 0:1]
    if use_sink:
        extra = jnp.exp(sink_ref[...] - m_ref[...])
        l_e = l_e + extra[:half]
        l_o = l_o + extra[half:]
    o_e = acc_e / l_e
    o_o = acc_o / l_o
    lo_half_q = lax.broadcasted_iota(jnp.int32, (tq, LANES), 1) < HEAD_DIM
    for p_ in range(nh // 2):
        pair = jnp.where(lo_half_q, o_e[p_ * tq:(p_ + 1) * tq, :], o_o[p_ * tq:(p_ + 1) * tq, :])
        o_ref[:, p_ * LANES:(p_ + 1) * LANES] = pair.astype(o_ref.dtype)


def _window_attn(proj, slopes, B, S, *, q_blk, kv_blk, window, sinks=None, out_dtype, tq, name):
    nq = S // tq
    G, nh = 2, 4
    back = -(-window // LANES) * LANES
    n_special = -(-back // tq)
    span = back + tq
    R = nh * tq
    assert span <= S
    kern = functools.partial(_window_kernel, tq=tq, back=back, window=window,
                             use_sink=sinks is not None, nh=nh, n_special=n_special)
    in_specs = [_smem_spec()]
    args = [slopes]
    if sinks is not None:
        in_specs.append(_smem_spec())
        args.append(sinks)
    in_specs += [
        pl.BlockSpec((tq, 2 * LANES), lambda b, g, i: (b * nq + i, q_blk // 2 + g)),
        pl.BlockSpec((S, LANES), lambda b, g, i: (b, kv_blk + g)),
        pl.BlockSpec((S, LANES), lambda b, g, i: (b, kv_blk + 2 + g)),
    ]
    args += [proj, proj, proj]
    return pl.pallas_call(
        kern,
        grid=(B, G, nq),
        in_specs=in_specs,
        out_specs=pl.BlockSpec((tq, 2 * LANES), lambda b, g, i: (b * nq + i, g)),
        out_shape=jax.ShapeDtypeStruct((B * S, BRANCH_WIDTH), out_dtype),
        scratch_shapes=[pltpu.VMEM((n_special + 1, R, span), F32),
                        pltpu.VMEM((R, LANES), BF16),
                        pltpu.VMEM((R, span), F32),
                        pltpu.VMEM((R, span), BF16),
                        pltpu.VMEM((R, 1), F32),
                        pltpu.VMEM((R, 1), F32)],
        compiler_params=_cparams(("parallel", "parallel", "arbitrary")),
        name=name,
    )(*args)


def _smem_spec():
    return pl.BlockSpec(memory_space=pltpu.SMEM)


def _diff_attn(proj, slopes, lam, subg, B, S, lambda_init, *, tq, tk):
    nq = S // tq
    kern = functools.partial(_attn_kernel, mode="diff", tq=tq, tk=tk, use_sel=False, nh=2,
                             lambda_init=lambda_init)
    return pl.pallas_call(
        kern,
        grid=(B, DIFF_HEADS, nq),
        in_specs=[_smem_spec(),
                  pl.BlockSpec((tq, LANES), lambda b, h, i: (b * nq + i, BLK_AQ + h)),
                  pl.BlockSpec((S, LANES), lambda b, h, i: (b, BLK_AK + h)),
                  pl.BlockSpec((S, LANES), lambda b, h, i: (b, BLK_AV + h)),
                  pl.BlockSpec((S, LANES), lambda b, h, i: (0, 0)),
                  pl.BlockSpec((4, HEAD_DIM), lambda b, h, i: (0, 0)),
                  pl.BlockSpec((1, LANES), lambda b, h, i: (0, 0))],
        out_specs=pl.BlockSpec((tq, LANES), lambda b, h, i: (b * nq + i, h)),
        out_shape=jax.ShapeDtypeStruct((B * S, BRANCH_WIDTH), BF16),
        scratch_shapes=_attn_scratch(tk // tq, 2 * tq, tk, 2 * LANES),
        compiler_params=_cparams(("parallel", "parallel", "arbitrary")),
        name="diff_attn",
    )(slopes, proj, proj, proj, _key_extras(S, tk), lam, subg)


def _slc_attn(proj, slopes, sel, B, S, *, tq, tk):
    nq = S // tq
    G, nh = 2, 4
    kern = functools.partial(_attn_kernel, mode="gqa", tq=tq, tk=tk, use_sel=True, nh=nh,
                             lambda_init=0.0)
    in_specs = [
        _smem_spec(),
        pl.BlockSpec((tq, 2 * LANES), lambda b, g, i: (b * nq + i, BLK_BQ // 2 + g)),
        pl.BlockSpec((S, LANES), lambda b, g, i: (b, BLK_SLC + g)),
        pl.BlockSpec((S, LANES), lambda b, g, i: (b, BLK_SLC + 2 + g)),
        pl.BlockSpec((S, LANES), lambda b, g, i: (0, 0)),
        pl.BlockSpec((tq, LANES), lambda b, g, i: (b * nq + i, g)),
    ]
    return pl.pallas_call(
        kern,
        grid=(B, G, nq),
        in_specs=in_specs,
        out_specs=pl.BlockSpec((tq, 2 * LANES), lambda b, g, i: (b * nq + i, g)),
        out_shape=jax.ShapeDtypeStruct((B * S, BRANCH_WIDTH), F32),
        scratch_shapes=_attn_scratch(tk // tq, nh * tq, tk, LANES),
        compiler_params=_cparams(("parallel", "parallel", "arbitrary")),
        name="nsa_slc_attn",
    )(slopes, proj, proj, proj, _key_extras(S, tk), sel)


def _cmp_attn_kernel(slopes_ref, q_ref, kc_ref, vc_ref, o_ref, sel_ref, *, tq, n_cmp, n_blk, n_sel):
    g = pl.program_id(1)
    i = pl.program_id(2)
    nh = NSA_HEADS // NSA_KV_GROUPS
    R = nh * tq
    qs = _stack_heads([q_ref[:, p * LANES:(p + 1) * LANES] for p in range(nh // 2)])
    slope_col = _head_column([slopes_ref[g * nh + h] for h in _stacked_head_order(nh)], tq)

    t = i * tq + (lax.broadcasted_iota(jnp.int32, (R, LANES), 0) & (tq - 1))
    c = lax.broadcasted_iota(jnp.int32, (R, LANES), 1)
    dist = t - (c * CMP_STRIDE + (CMP_LEN - 1))
    valid = (dist >= 0) & (c < n_cmp)
    s = _dot_nt(qs, kc_ref[...])
    s = jnp.where(valid, s - slope_col * dist.astype(F32), NEG_INF)
    m = jnp.max(s, axis=-1, keepdims=True)
    e = jnp.exp(s - m)
    p_c = e / jnp.sum(e, axis=-1, keepdims=True)
    any_valid = (t[:, 0:1] >= CMP_LEN - 1).astype(F32)
    p_c = p_c * any_valid
    o = _dot(p_c.astype(BF16), vc_ref[...])
    lane = lax.broadcasted_iota(jnp.int32, (tq, LANES), 1)
    lo_half = lane < HEAD_DIM
    for p in range(nh // 2):
        pair = jnp.where(lo_half, o[p * tq:(p + 1) * tq, :],
                         o[(nh // 2 + p) * tq:(nh // 2 + p + 1) * tq, :])
        o_ref[:, p * LANES:(p + 1) * LANES] = pair.astype(o_ref.dtype)

    p_sum = p_c[0:tq, :]
    for h in range(1, nh):
        p_sum = p_sum + p_c[h * tq:(h + 1) * tq, :]
    jj = lax.broadcasted_iota(jnp.int32, (n_blk, LANES), 0)
    cc = lax.broadcasted_iota(jnp.int32, (n_blk, LANES), 1)
    overlap = (jnp.minimum(cc * CMP_STRIDE + CMP_LEN, jj * SLC_BLOCK + SLC_BLOCK)
               - jnp.maximum(cc * CMP_STRIDE, jj * SLC_BLOCK))
    overlap = jnp.where(cc < n_cmp, jnp.maximum(overlap, 0), 0)
    w_t = (overlap.astype(F32) * (1.0 / CMP_LEN)).astype(BF16)
    hi = p_sum.astype(BF16)
    rem = p_sum - hi.astype(F32)
    mid = rem.astype(BF16)
    low = (rem - mid.astype(F32)).astype(BF16)
    score = _dot_nt(w_t, hi) + _dot_nt(w_t, mid) + _dot_nt(w_t, low)

    tpos = i * tq + lax.broadcasted_iota(jnp.int32, (n_blk, tq), 1)
    t_blk = tpos >> _log2(SLC_BLOCK)
    jb = lax.broadcasted_iota(jnp.int32, (n_blk, tq), 0)
    score = jnp.where((jb == 0) | (jb == t_blk) | (jb == t_blk - 1), FORCED_SCORE, score)
    score = jnp.where(jb > t_blk, -1.0, score)
    rank = jnp.zeros((n_blk, tq), F32)
    for kk in range(n_blk):
        row = score[kk:kk + 1, :]
        ahead = (row > score) | ((row == score) & (kk < jb))
        rank = rank + ahead.astype(F32)
    sel_t = (rank < n_sel).astype(BF16)
    sel_t = jnp.concatenate([sel_t, jnp.zeros((LANES - n_blk, tq), BF16)], axis=0)
    eye = (lax.broadcasted_iota(jnp.int32, (tq, tq), 0)
           == lax.broadcasted_iota(jnp.int32, (tq, tq), 1)).astype(BF16)
    sel_ref[...] = _dot_nt(eye, sel_t).astype(sel_ref.dtype)


def _cmp_attn(proj, kvc, slopes, B, S, *, tq=256):
    nq = S // tq
    G = NSA_KV_GROUPS
    n_cmp = (S - CMP_LEN) // CMP_STRIDE + 1
    n_blk = S // SLC_BLOCK
    n_sel = min(SLC_TOPK, n_blk)
    kern = functools.partial(_cmp_attn_kernel, tq=tq, n_cmp=n_cmp, n_blk=n_blk, n_sel=n_sel)
    return pl.pallas_call(
        kern,
        grid=(B, G, nq),
        in_specs=[_smem_spec(),
                  pl.BlockSpec((tq, 2 * LANES), lambda b, g, i: (b * nq + i, BLK_BQ // 2 + g)),
                  pl.BlockSpec((None, LANES, LANES), lambda b, g, i: (0, b * G + g, 0)),
                  pl.BlockSpec((None, LANES, LANES), lambda b, g, i: (1, b * G + g, 0))],
        out_specs=[pl.BlockSpec((tq, 2 * LANES), lambda b, g, i: (b * nq + i, g)),
                   pl.BlockSpec((tq, LANES), lambda b, g, i: (b * nq + i, g))],
        out_shape=[jax.ShapeDtypeStruct((B * S, BRANCH_WIDTH), F32),
                   jax.ShapeDtypeStruct((B * S, G * LANES), BF16)],
        compiler_params=_cparams(("parallel", "parallel", "parallel")),
        name="nsa_cmp_attn",
    )(slopes, proj, kvc, kvc)


def _interleave_matrix(tm, n_outer):
    e_cnt = tm // n_outer
    r_out = lax.broadcasted_iota(jnp.int32, (tm, tm), 0)
    r_in = lax.broadcasted_iota(jnp.int32, (tm, tm), 1)
    outer = r_in >> _log2(e_cnt)
    inner = r_in & (e_cnt - 1)
    return (r_out == inner * n_outer + outer).astype(BF16)


def _permute_rows(pm, x):
    if x.dtype == BF16:
        return _dot(pm, x)
    hi = x.astype(BF16)
    rem = x - hi.astype(F32)
    mid = rem.astype(BF16)
    low = (rem - mid.astype(F32)).astype(BF16)
    return _dot(pm, hi) + _dot(pm, mid) + _dot(pm, low)


def _merge_kernel(x_ref, oa_ref, ocmp_ref, oslc_ref, owin_ref, oc_ref, gts_ref,
                  wmg_ref, wbr_ref, wout_ref, g_ref, b_ref, o_ref):
    tm = x_ref.shape[0]
    x = x_ref[...]
    xb = x.astype(BF16)
    p_slc = _interleave_matrix(tm, oslc_ref.shape[0])
    p_band = _interleave_matrix(tm, owin_ref.shape[0])
    o_slc = _permute_rows(p_slc, oslc_ref[...].reshape(tm, BRANCH_WIDTH))
    o_win = _permute_rows(p_band, owin_ref[...].reshape(tm, BRANCH_WIDTH))
    o_c = _permute_rows(p_band, oc_ref[...].reshape(tm, BRANCH_WIDTH)).astype(BF16)
    gates = jax.nn.sigmoid(gts_ref[...])
    lane = lax.broadcasted_iota(jnp.int32, (tm, LANES), 1)
    lo_half = lane < HEAD_DIM
    nsa = (ocmp_ref[...], o_slc, o_win)
    ob_tiles = []
    for pair in range(NSA_HEADS // 2):
        acc = jnp.zeros((tm, LANES), F32)
        for n in range(3):
            c0 = n * NSA_HEADS + 2 * pair
            gate = jnp.where(lo_half, gates[:, c0:c0 + 1], gates[:, c0 + 1:c0 + 2])
            acc = acc + gate * nsa[n][:, pair * LANES:(pair + 1) * LANES]
        ob_tiles.append(acc.astype(BF16))
    ob = jnp.concatenate(ob_tiles, axis=1)
    branches = (oa_ref[...], ob, o_c)
    y = jnp.zeros((tm, D_MODEL), F32)
    for n in range(N_BRANCHES):
        mg = jax.nn.sigmoid(_dot(xb, wmg_ref[:, n * D_MODEL:(n + 1) * D_MODEL]))
        y = y + mg * _dot(branches[n], wbr_ref[n])
    mix = _dot(y.astype(BF16), wout_ref[...])
    o_ref[...] = _layer_norm(ALPHA * x + mix, g_ref[...], b_ref[...])


def _merge(x, o_a, o_cmp, o_slc, o_win, o_c, projf, w_mg, w_br, w_out, ln_g, ln_b, B, S,
           *, tm=512):
    T, D = x.shape
    nt = S // tm
    n_slc, n_band = S // SLC_QUERY_CHUNK, S // QUERY_BLOCK
    o_slc = o_slc.reshape(B, n_slc, SLC_QUERY_CHUNK, BRANCH_WIDTH)
    o_win = o_win.reshape(B, n_band, QUERY_BLOCK, BRANCH_WIDTH)
    o_c = o_c.reshape(B, n_band, QUERY_BLOCK, BRANCH_WIDTH)
    row = lambda b, a: (b * nt + a, 0)
    const2 = lambda b, a: (0, 0)
    chunked = lambda b, a: (b, 0, a, 0)
    return pl.pallas_call(
        _merge_kernel,
        grid=(B, nt),
        in_specs=[pl.BlockSpec((tm, D), row),
                  pl.BlockSpec((tm, BRANCH_WIDTH), row),
                  pl.BlockSpec((tm, BRANCH_WIDTH), row),
                  pl.BlockSpec((None, n_slc, tm // n_slc, BRANCH_WIDTH), chunked),
                  pl.BlockSpec((None, n_band, tm // n_band, BRANCH_WIDTH), chunked),
                  pl.BlockSpec((None, n_band, tm // n_band, BRANCH_WIDTH), chunked),
                  pl.BlockSpec((tm, LANES), lambda b, a: (b * nt + a, 2)),
                  pl.BlockSpec((D, N_BRANCHES * D), const2),
                  pl.BlockSpec((N_BRANCHES, BRANCH_WIDTH, D), lambda b, a: (0, 0, 0)),
                  pl.BlockSpec((D, D), const2),
                  pl.BlockSpec((1, D), const2),
                  pl.BlockSpec((1, D), const2)],
        out_specs=pl.BlockSpec((tm, D), row),
        out_shape=jax.ShapeDtypeStruct((T, D), F32),
        compiler_params=_cparams(("parallel", "parallel")),
        name="merge",
    )(x, o_a, o_cmp, o_slc, o_win, o_c, projf, w_mg, w_br, w_out, ln_g, ln_b)


def _dup_kv(w, groups):
    D = w.shape[0]
    w = w.reshape(D, 2 * groups, 1, HEAD_DIM)
    return jnp.broadcast_to(w, (D, 2 * groups, 2, HEAD_DIM)).reshape(D, 4 * groups * HEAD_DIM)


def _split_w_in(w):
    sizes = (512, 512, 512, 512, 256, 256, 256, 24, 512, 128, 128, 3072)
    offs = [0]
    for n in sizes:
        offs.append(offs[-1] + n)
    (a_q, a_k, a_v, b_q, b_kvc, b_kvs, b_kvw, b_g, c_q, c_k, c_v, m_g) = [
        w[:, offs[n]:offs[n + 1]] for n in range(len(sizes))]
    w_bf = jnp.concatenate(
        [a_q, a_k, a_v, b_q, c_q, _dup_kv(b_kvs, 2), _dup_kv(b_kvw, 2),
         _dup_kv(jnp.concatenate([c_k, c_v], axis=1), 2)], axis=1).astype(BF16)
    pad = jnp.zeros((w.shape[0], N_PROJ_F32 - 256 - 24), w.dtype)
    w_f32 = jnp.concatenate([b_kvc, b_g, pad], axis=1).astype(BF16)
    return w_bf, w_f32, m_g.astype(BF16)


def _alibi_slopes(n_heads):
    assert 8 % n_heads == 0
    return jnp.exp2(-8.0 * jnp.arange(1, n_heads + 1, dtype=F32) / n_heads)


def _token_mixing(x, B, S, w_in, diff_lam, diff_subln_g, cmp_pos, cmp_w1, cmp_w2, sinks,
                  w_branch, w_out, ln_g, ln_b, lambda_init):
    G = NSA_KV_GROUPS
    w_bf, w_f32, w_mg = _split_w_in(w_in)
    proj, projf = _proj(x, w_bf, w_f32)

    n_rows = S // CMP_STRIDE
    r = projf[:, :4 * HEAD_DIM].reshape(B, S, 2, G, HEAD_DIM).transpose(2, 0, 3, 1, 4)
    r = r.reshape(2, B * G * n_rows, CMP_STRIDE * HEAD_DIM)
    pos = cmp_pos.reshape(2, 2, CMP_STRIDE * HEAD_DIM)
    w2dup = jnp.concatenate([cmp_w2, cmp_w2], axis=-1).astype(BF16)
    kvc = _compress(r, pos, cmp_w1.astype(BF16), w2dup)

    slopes8 = _alibi_slopes(NSA_HEADS)
    o_a = _diff_attn(proj, _alibi_slopes(DIFF_HEADS), diff_lam, diff_subln_g.reshape(1, LANES),
                     B, S, lambda_init, tq=512, tk=512)
    o_cmp, sel = _cmp_attn(proj, kvc, slopes8, B, S)
    o_slc = _slc_attn(proj, slopes8, sel, B, S, tq=512, tk=512)
    o_win = _window_attn(proj, slopes8, B, S, q_blk=BLK_BQ, kv_blk=BLK_WIN, window=NSA_WINDOW,
                         out_dtype=F32, tq=256, name="nsa_win_attn")
    o_c = _window_attn(proj, _alibi_slopes(SWA_HEADS), B, S, q_blk=BLK_CQ, kv_blk=BLK_SWA,
                       window=SWA_WINDOW, sinks=sinks, out_dtype=BF16, tq=256, name="swa_attn")
    return _merge(x, o_a, o_cmp, o_slc, o_win, o_c, projf, w_mg, w_branch.astype(BF16),
                  w_out.astype(BF16), ln_g, ln_b, B, S)


def kernel(x, p, ffn_w_in, ffn_w_out, ln_g, ln_b, w_in, diff_lam, diff_subln_g, nsa_cmp_pos,
           nsa_cmp_w1, nsa_cmp_w2, swa_sinks, w_branch, w_out, ple_w_in, ple_w_gate):
    B, S, D = x.shape
    T = B * S
    h = x.reshape(T, D)
    for i in range(DEPTH):
        lambda_init = 0.8 - 0.6 * math.exp(-0.3 * i)
        lg = ln_g[i].reshape(3, 1, D)
        lb = ln_b[i].reshape(3, 1, D)
        h = _ffn(h, ffn_w_in[i, 0].astype(BF16), ffn_w_out[i, 0].astype(BF16), lg[0], lb[0])
        h = _token_mixing(h, B, S, w_in[i], diff_lam[i], diff_subln_g[i], nsa_cmp_pos[i],
                          nsa_cmp_w1[i], nsa_cmp_w2[i], swa_sinks[i], w_branch[i], w_out[i],
                          lg[1], lb[1], lambda_init)
        h = _ffn(h, ffn_w_in[i, 1].astype(BF16), ffn_w_out[i, 1].astype(BF16), lg[2], lb[2],
                 ple_args=(p[i].reshape(T, PLE_DIM), ple_w_gate[i].astype(BF16),
                           ple_w_in[i].astype(BF16)))
    return h.reshape(B, S, D)
```

```python
import functools
import math

import jax
import jax.numpy as jnp
from jax import lax
from jax.experimental import pallas as pl
from jax.experimental.pallas import tpu as pltpu

F32 = jnp.float32
BF16 = jnp.bfloat16

D_MODEL = 1024
DEPTH = 2
HEAD_DIM = 64
DIFF_HEADS = 4
NSA_HEADS = 8
NSA_KV_GROUPS = 2
CMP_LEN = 32
CMP_STRIDE = 16
CMP_HIDDEN = 256
SLC_BLOCK = 64
SLC_TOPK = 16
NSA_WINDOW = 512
FORCED_SCORE = 1e4
SWA_HEADS = 8
SWA_KV_HEADS = 2
SWA_WINDOW = 128
QUERY_BLOCK = 128
SLC_QUERY_CHUNK = 32
N_BRANCHES = 3
BRANCH_WIDTH = 512
D_FF = 2816
PLE_DIM = 256
LN_EPS = 1e-5
SUBLN_EPS = 1e-5
ALPHA = (2 * DEPTH) ** 0.25
NEG_INF = -1e30
QK_SCALE = HEAD_DIM ** -0.5

LANES = 128
VMEM_LIMIT = 56 * 1024 * 1024

BLK_AQ, BLK_AK, BLK_AV = 0, 4, 8
BLK_BQ, BLK_CQ = 12, 16
BLK_SLC, BLK_WIN, BLK_SWA = 20, 24, 28
N_PROJ_BF = 32 * LANES
N_PROJ_F32 = 3 * LANES


def _cparams(sem):
    return pltpu.CompilerParams(dimension_semantics=sem, vmem_limit_bytes=VMEM_LIMIT)


def _layer_norm(h, g, b):
    mu = jnp.mean(h, axis=-1, keepdims=True)
    d = h - mu
    var = jnp.mean(d * d, axis=-1, keepdims=True)
    return d * lax.rsqrt(var + LN_EPS) * g + b


def _dot(a, b):
    return jnp.dot(a, b, preferred_element_type=F32)


def _dot_nt(a, b):
    return lax.dot_general(a, b, (((1,), (1,)), ((), ())), preferred_element_type=F32)


FF_CHUNK = 256
EPILOGUE_ROWS = 256


def _ffn_kernel(*refs, ple):
    if ple:
        x_ref, win_ref, wout_ref, g_ref, b_ref, p_ref, pwg_ref, pwi_ref, o_ref, acc_ref = refs
    else:
        x_ref, win_ref, wout_ref, g_ref, b_ref, o_ref, acc_ref = refs
    xb = x_ref[...].astype(BF16)
    for k in range(D_FF // FF_CHUNK):
        c0 = k * FF_CHUNK
        gate = _dot(xb, win_ref[:, c0:c0 + FF_CHUNK])
        up = _dot(xb, win_ref[:, D_FF + c0:D_FF + c0 + FF_CHUNK])
        hid = ((gate * jax.nn.sigmoid(gate)) * up).astype(BF16)
        contrib = _dot(hid, wout_ref[c0:c0 + FF_CHUNK, :])
        if k == 0:
            acc_ref[...] = contrib
        else:
            acc_ref[...] += contrib
    for r0 in range(0, x_ref.shape[0], EPILOGUE_ROWS):
        rows = slice(r0, r0 + EPILOGUE_ROWS)
        h = ALPHA * x_ref[rows, :] + 0.5 * acc_ref[rows, :]
        if ple:
            gp = jax.nn.sigmoid(_dot(h.astype(BF16), pwg_ref[...]))
            h = h + gp * _dot(p_ref[rows, :].astype(BF16), pwi_ref[...])
        o_ref[rows, :] = _layer_norm(h, g_ref[...], b_ref[...])


def _resident(shape):
    return pl.BlockSpec(shape, lambda i: (0,) * len(shape), pipeline_mode=pl.Buffered(1))


def _ffn(x, w_in, w_out, ln_g, ln_b, ple_args=None, *, tm=512):
    T, D = x.shape
    ple = ple_args is not None
    in_specs = [pl.BlockSpec((tm, D), lambda i: (i, 0)),
                _resident(w_in.shape), _resident(w_out.shape),
                _resident((1, D)), _resident((1, D))]
    args = [x, w_in, w_out, ln_g, ln_b]
    if ple:
        p, pwg, pwi = ple_args
        in_specs += [pl.BlockSpec((tm, PLE_DIM), lambda i: (i, 0)),
                     _resident(pwg.shape), _resident(pwi.shape)]
        args += [p, pwg, pwi]
    return pl.pallas_call(
        functools.partial(_ffn_kernel, ple=ple),
        grid=(T // tm,),
        in_specs=in_specs,
        out_specs=pl.BlockSpec((tm, D), lambda i: (i, 0)),
        out_shape=jax.ShapeDtypeStruct((T, D), F32),
        scratch_shapes=[pltpu.VMEM((tm, D), F32)],
        compiler_params=_cparams(("parallel",)),
        name="ffn_ple" if ple else "ffn",
    )(*args)


PROJ_COLS = 1024


def _proj_kernel(x_ref, wb_ref, wf_ref, ob_ref, of_ref):
    xb = x_ref[...].astype(BF16)
    for n0 in range(0, wb_ref.shape[1], PROJ_COLS):
        ob_ref[:, n0:n0 + PROJ_COLS] = _dot(xb, wb_ref[:, n0:n0 + PROJ_COLS]).astype(BF16)
    of_ref[...] = _dot(xb, wf_ref[...])


def _proj(x, w_bf, w_f32, *, tm=512):
    T, K = x.shape
    nb, nf = w_bf.shape[1], w_f32.shape[1]
    return pl.pallas_call(
        _proj_kernel,
        grid=(T // tm,),
        in_specs=[pl.BlockSpec((tm, K), lambda i: (i, 0)),
                  pl.BlockSpec((K, nb), lambda i: (0, 0)),
                  pl.BlockSpec((K, nf), lambda i: (0, 0))],
        out_specs=[pl.BlockSpec((tm, nb), lambda i: (i, 0)),
                   pl.BlockSpec((tm, nf), lambda i: (i, 0))],
        out_shape=[jax.ShapeDtypeStruct((T, nb), BF16), jax.ShapeDtypeStruct((T, nf), F32)],
        compiler_params=_cparams(("parallel",)),
        name="proj",
    )(x, w_bf, w_f32)


def _compress_kernel(r_ref, pos_ref, w1_ref, w2_ref, o_ref):
    r = r_ref[...]
    half = CMP_STRIDE * HEAD_DIM
    lo = _dot((r + pos_ref[0:1, :]).astype(BF16), w1_ref[0:half, :])
    hi = _dot((r + pos_ref[1:2, :]).astype(BF16), w1_ref[half:2 * half, :])
    pre = lo + pltpu.roll(hi, hi.shape[0] - 1, 0)
    hid = jax.nn.gelu(pre, approximate=True)
    o_ref[...] = _dot(hid.astype(BF16), w2_ref[...]).astype(o_ref.dtype)


def _compress(r, pos, w1, w2dup, *, tm=512):
    _, M, F = r.shape
    tm = min(tm, M)
    return pl.pallas_call(
        _compress_kernel,
        grid=(2, M // tm),
        in_specs=[pl.BlockSpec((None, tm, F), lambda n, i: (n, i, 0)),
                  pl.BlockSpec((None, 2, F), lambda n, i: (n, 0, 0)),
                  pl.BlockSpec((None, 2 * F, CMP_HIDDEN), lambda n, i: (n, 0, 0)),
                  pl.BlockSpec((None, CMP_HIDDEN, LANES), lambda n, i: (n, 0, 0))],
        out_specs=pl.BlockSpec((None, tm, LANES), lambda n, i: (n, i, 0)),
        out_shape=jax.ShapeDtypeStruct((2, M, LANES), BF16),
        compiler_params=_cparams(("parallel", "parallel")),
        name="nsa_compress",
    )(r, pos, w1, w2dup)


def _log2(n):
    assert n > 0 and n & (n - 1) == 0, n
    return n.bit_length() - 1


def _stack_heads(q_tiles):
    tq = q_tiles[0].shape[0]
    lane = lax.broadcasted_iota(jnp.int32, (tq, LANES), 1)
    lo = lane < HEAD_DIM
    zero = jnp.zeros((tq, LANES), BF16)
    scaled = [q * jnp.asarray(QK_SCALE, BF16) for q in q_tiles]
    parts = [jnp.where(lo, qs, zero) for qs in scaled] + [jnp.where(lo, zero, qs) for qs in scaled]
    return jnp.concatenate(parts, axis=0)


def _stacked_head_order(nh):
    return list(range(0, nh, 2)) + list(range(1, nh, 2))


def _head_column(values, tq):
    nh = len(values)
    head = lax.broadcasted_iota(jnp.int32, (nh * tq, 1), 0) >> _log2(tq)
    col = jnp.zeros((nh * tq, 1), F32)
    for h in range(nh):
        col = jnp.where(head == h, values[h], col)
    return col


KX_ONE_A, KX_COL_HI, KX_COL_LO, KX_ONE_B = 32, 33, 34, 35
POS_SPLIT = 16


def _key_extras(S, tk):
    assert S // SLC_BLOCK <= KX_ONE_A
    key = jnp.arange(S, dtype=jnp.int32)[:, None]
    lane = jnp.arange(LANES, dtype=jnp.int32)[None, :]
    col = key % tk
    x = jnp.where(lane == key // SLC_BLOCK, -NEG_INF, 0.0)
    x = jnp.where((lane == KX_ONE_A) | (lane == KX_ONE_B), 1.0, x)
    x = jnp.where(lane == KX_COL_HI, (col - col % POS_SPLIT).astype(F32), x)
    x = jnp.where(lane == KX_COL_LO, (col % POS_SPLIT).astype(F32), x)
    return x.astype(BF16)


SOFTMAX_ROWS = 32


def _attn_kernel(*refs, mode, tq, tk, use_sel, nh, lambda_init):
    refs = list(refs)
    slopes_ref = refs.pop(0)
    q_ref, k_ref, v_ref, kx_ref = refs.pop(0), refs.pop(0), refs.pop(0), refs.pop(0)
    sel_ref = refs.pop(0) if use_sel else None
    if mode == "diff":
        lam_ref, subg_ref = refs.pop(0), refs.pop(0)
    o_ref = refs.pop(0)
    mask_ref, qs_ref, s_ref, p_ref, m_ref, acc_ref, shift_ref = refs
    hg = pl.program_id(1)
    i = pl.program_id(2)
    R = nh * tq
    gqa = mode == "gqa"
    ratio = tk // tq
    half = R // 2

    xl = lax.broadcasted_iota(jnp.int32, (R, LANES), 1)

    @pl.when(i == 0)
    def _():
        if gqa:
            slope_col = _head_column([slopes_ref[hg * nh + h] for h in _stacked_head_order(nh)], tq)
        else:
            slope_col = jnp.full((R, 1), slopes_ref[hg], F32)
        shift_ref[...] = slope_col * float(tk)
        row = lax.broadcasted_iota(jnp.int32, (R, tk), 0) & (tq - 1)
        col = lax.broadcasted_iota(jnp.int32, (R, tk), 1)
        for a in range(ratio):
            mask_ref[a] = jnp.where(row - col + a * tq >= 0, 0.0, NEG_INF)
        xr = lax.broadcasted_iota(jnp.int32, (R, LANES), 0) & (tq - 1)
        xr_lo = xr & (POS_SPLIT - 1)
        qx = jnp.where(xl == KX_ONE_A, -slope_col * (xr - xr_lo).astype(F32), 0.0)
        qx = jnp.where(xl == KX_ONE_B, -slope_col * xr_lo.astype(F32), qx)
        qx = jnp.where((xl == KX_COL_HI) | (xl == KX_COL_LO), slope_col, qx)
        qs_ref[:, LANES:2 * LANES] = qx.astype(BF16)

    if gqa:
        qs_ref[:, 0:LANES] = _stack_heads([q_ref[:, p * LANES:(p + 1) * LANES]
                                           for p in range(nh // 2)])
    else:
        qs_ref[:, 0:LANES] = _stack_heads([q_ref[...]])
    if use_sel:
        n_blk = k_ref.shape[0] // SLC_BLOCK
        not_sel = jnp.concatenate([(sel_ref[...].astype(F32) - 1.0).astype(BF16)] * nh, axis=0)
        qs_ref[:, LANES:2 * LANES] = jnp.where(xl < n_blk, not_sel, qs_ref[:, LANES:2 * LANES])
    m_ref[...] = jnp.full((R, 1), NEG_INF, F32)
    acc_ref[...] = jnp.zeros(acc_ref.shape, F32)
    lo_half_k = lax.broadcasted_iota(jnp.int32, (tk, LANES), 1) < HEAD_DIM

    def scores(j, slot):
        start = pl.multiple_of(j * tk, tk)
        kk = jnp.concatenate([k_ref[pl.ds(start, tk), :], kx_ref[pl.ds(start, tk), :]], axis=1)
        for h0 in (0, half):
            s_ref[slot, h0:h0 + half, :] = _dot_nt(qs_ref[h0:h0 + half, :], kk)

    def softmax_pv(j, slot, mask_idx):
        start = pl.multiple_of(j * tk, tk)
        v = v_ref[pl.ds(start, tk), :]
        one = jnp.ones((tk, LANES), BF16)
        if gqa:
            v_halves = (jnp.where(lo_half_k, v, one), jnp.where(lo_half_k, one, v))
        else:
            v_halves = (jnp.concatenate([v, one], axis=1),) * 2
        for hh, h0 in enumerate((0, half)):
            for c in range(half // SOFTMAX_ROWS):
                r0 = h0 + c * SOFTMAX_ROWS
                rows = slice(r0, r0 + SOFTMAX_ROWS)
                s = s_ref[slot, rows, :]
                if mask_idx is not None:
                    s = s + mask_ref[mask_idx, rows, :]
                m_prev = m_ref[rows, :] - shift_ref[rows, :]
                m_new = jnp.maximum(m_prev, jnp.max(s, axis=-1, keepdims=True))
                alpha = jnp.exp(m_prev - m_new)
                p_ref[rows, :] = jnp.exp(s - m_new).astype(BF16)
                m_ref[rows, :] = m_new
                acc_ref[rows, :] = alpha * acc_ref[rows, :]
            acc_ref[h0:h0 + half, :] += _dot(p_ref[h0:h0 + half, :], v_halves[hh])

    j_last = i // ratio
    scores(0, 0)

    def pair(t, c):
        scores(2 * t + 1, 1)
        softmax_pv(2 * t, 0, None)
        scores(2 * t + 2, 0)
        softmax_pv(2 * t + 1, 1, None)
        return c

    lax.fori_loop(0, j_last // 2, pair, 0)

    @pl.when(j_last % 2 == 1)
    def _():
        scores(j_last, 1)
        softmax_pv(j_last - 1, 0, None)
        softmax_pv(j_last, 1, i % ratio)

    @pl.when(j_last % 2 == 0)
    def _():
        softmax_pv(j_last, 0, i % ratio)

    acc = acc_ref[...]
    if gqa:
        acc_e, acc_o = acc[:half], acc[half:]
        l_e = acc_e[:, HEAD_DIM:HEAD_DIM + 1]
        l_o = acc_o[:, 0:1]
        o_e = acc_e / l_e
        o_o = acc_o / l_o
        lo_half_q = lax.broadcasted_iota(jnp.int32, (tq, LANES), 1) < HEAD_DIM
        for p_ in range(nh // 2):
            pair = jnp.where(lo_half_q, o_e[p_ * tq:(p_ + 1) * tq, :], o_o[p_ * tq:(p_ + 1) * tq, :])
            o_ref[:, p_ * LANES:(p_ + 1) * LANES] = pair.astype(o_ref.dtype)
    else:
        o = acc[:, 0:LANES] / acc[:, LANES:LANES + 1]
        lam = lam_ref[...]
        lam_full = (jnp.exp(jnp.sum(lam[0:1, :] * lam[1:2, :], axis=-1, keepdims=True))
                    - jnp.exp(jnp.sum(lam[2:3, :] * lam[3:4, :], axis=-1, keepdims=True))
                    + lambda_init)
        od = o[0:tq, :] - lam_full * o[tq:2 * tq, :]
        od = od * lax.rsqrt(jnp.mean(od * od, axis=-1, keepdims=True) + SUBLN_EPS) * subg_ref[...]
        o_ref[...] = (od * (1.0 - lambda_init)).astype(o_ref.dtype)


def _attn_scratch(n_masks, R, tk, acc_width):
    return [pltpu.VMEM((n_masks, R, tk), F32),
            pltpu.VMEM((R, 2 * LANES), BF16),
            pltpu.VMEM((2, R, tk), F32),
            pltpu.VMEM((R, tk), BF16),
            pltpu.VMEM((R, 1), F32),
            pltpu.VMEM((R, acc_width), F32),
            pltpu.VMEM((R, 1), F32)]


def _window_kernel(*refs, tq, back, window, use_sink, nh, n_special):
    refs = list(refs)
    slopes_ref = refs.pop(0)
    sinks_ref = refs.pop(0) if use_sink else None
    q_ref, k_ref, v_ref, o_ref, bias_ref, qs_ref, s_ref, p_ref, m_ref, sink_ref = refs
    g = pl.program_id(1)
    i = pl.program_id(2)
    R = nh * tq
    half = R // 2
    span = back + tq
    order = _stacked_head_order(nh)

    @pl.when(i == 0)
    def _():
        slope_col = _head_column([slopes_ref[g * nh + h] for h in order], tq)
        if use_sink:
            sink_ref[...] = _head_column([sinks_ref[g * nh + h] for h in order], tq)
        row = lax.broadcasted_iota(jnp.int32, (R, span), 0) & (tq - 1)
        col = lax.broadcasted_iota(jnp.int32, (R, span), 1)
        for a in range(n_special + 1):
            dist = row - col + (a * tq if a < n_special else back)
            ok = (dist >= 0) & (dist < window)
            bias_ref[a] = jnp.where(ok, -slope_col * dist.astype(F32), NEG_INF)

    qs_ref[...] = _stack_heads([q_ref[:, p * LANES:(p + 1) * LANES] for p in range(nh // 2)])
    a = jnp.minimum(i, n_special)
    start = pl.multiple_of(jnp.maximum(i * tq - back, 0), LANES)
    k = k_ref[pl.ds(start, span), :]
    v = v_ref[pl.ds(start, span), :]
    lo_half_k = lax.broadcasted_iota(jnp.int32, (span, LANES), 1) < HEAD_DIM
    one = jnp.ones((span, LANES), BF16)
    v_halves = (jnp.where(lo_half_k, v, one), jnp.where(lo_half_k, one, v))
    for h0 in (0, half):
        s_ref[h0:h0 + half, :] = _dot_nt(qs_ref[h0:h0 + half, :], k)
    accs = []
    for hh, h0 in enumerate((0, half)):
        for c in range(half // SOFTMAX_ROWS):
            r0 = h0 + c * SOFTMAX_ROWS
            rows = slice(r0, r0 + SOFTMAX_ROWS)
            s = s_ref[rows, :] + bias_ref[a, rows, :]
            m = jnp.max(s, axis=-1, keepdims=True)
            p_ref[rows, :] = jnp.exp(s - m).astype(BF16)
            m_ref[rows, :] = m
        accs.append(_dot(p_ref[h0:h0 + half, :], v_halves[hh]))

    acc_e, acc_o = accs
    l_e = acc_e[:, HEAD_DIM:HEAD_DIM + 1]
    l_o = acc_o[:, 0:1]
    if use_sink:
        extra = jnp.exp(sink_ref[...] - m_ref[...])
        l_e = l_e + extra[:half]
        l_o = l_o + extra[half:]
    o_e = acc_e / l_e
    o_o = acc_o / l_o
    lo_half_q = lax.broadcasted_iota(jnp.int32, (tq, LANES), 1) < HEAD_DIM
    for p_ in range(nh // 2):
        pair = jnp.where(lo_half_q, o_e[p_ * tq:(p_ + 1) * tq, :], o_o[p_ * tq:(p_ + 1) * tq, :])
        o_ref[:, p_ * LANES:(p_ + 1) * LANES] = pair.astype(o_ref.dtype)


def _window_attn(proj, slopes, B, S, *, q_blk, kv_blk, window, sinks=None, out_dtype, tq, name):
    nq = S // tq
    G, nh = 2, 4
    back = -(-window // LANES) * LANES
    n_special = -(-back // tq)
    span = back + tq
    R = nh * tq
    assert span <= S
    kern = functools.partial(_window_kernel, tq=tq, back=back, window=window,
                             use_sink=sinks is not None, nh=nh, n_special=n_special)
    in_specs = [_smem_spec()]
    args = [slopes]
    if sinks is not None:
        in_specs.append(_smem_spec())
        args.append(sinks)
    in_specs += [
        pl.BlockSpec((tq, 2 * LANES), lambda b, g, i: (b * nq + i, q_blk // 2 + g)),
        pl.BlockSpec((S, LANES), lambda b, g, i: (b, kv_blk + g)),
        pl.BlockSpec((S, LANES), lambda b, g, i: (b, kv_blk + 2 + g)),
    ]
    args += [proj, proj, proj]
    return pl.pallas_call(
        kern,
        grid=(B, G, nq),
        in_specs=in_specs,
        out_specs=pl.BlockSpec((tq, 2 * LANES), lambda b, g, i: (b * nq + i, g)),
        out_shape=jax.ShapeDtypeStruct((B * S, BRANCH_WIDTH), out_dtype),
        scratch_shapes=[pltpu.VMEM((n_special + 1, R, span), F32),
                        pltpu.VMEM((R, LANES), BF16),
                        pltpu.VMEM((R, span), F32),
                        pltpu.VMEM((R, span), BF16),
                        pltpu.VMEM((R, 1), F32),
                        pltpu.VMEM((R, 1), F32)],
        compiler_params=_cparams(("parallel", "parallel", "arbitrary")),
        name=name,
    )(*args)


def _smem_spec():
    return pl.BlockSpec(memory_space=pltpu.SMEM)


def _diff_attn(proj, slopes, lam, subg, B, S, lambda_init, *, tq, tk):
    nq = S // tq
    kern = functools.partial(_attn_kernel, mode="diff", tq=tq, tk=tk, use_sel=False, nh=2,
                             lambda_init=lambda_init)
    return pl.pallas_call(
        kern,
        grid=(B, DIFF_HEADS, nq),
        in_specs=[_smem_spec(),
                  pl.BlockSpec((tq, LANES), lambda b, h, i: (b * nq + i, BLK_AQ + h)),
                  pl.BlockSpec((S, LANES), lambda b, h, i: (b, BLK_AK + h)),
                  pl.BlockSpec((S, LANES), lambda b, h, i: (b, BLK_AV + h)),
                  pl.BlockSpec((S, LANES), lambda b, h, i: (0, 0)),
                  pl.BlockSpec((4, HEAD_DIM), lambda b, h, i: (0, 0)),
                  pl.BlockSpec((1, LANES), lambda b, h, i: (0, 0))],
        out_specs=pl.BlockSpec((tq, LANES), lambda b, h, i: (b * nq + i, h)),
        out_shape=jax.ShapeDtypeStruct((B * S, BRANCH_WIDTH), BF16),
        scratch_shapes=_attn_scratch(tk // tq, 2 * tq, tk, 2 * LANES),
        compiler_params=_cparams(("parallel", "parallel", "arbitrary")),
        name="diff_attn",
    )(slopes, proj, proj, proj, _key_extras(S, tk), lam, subg)


def _slc_attn(proj, slopes, sel, B, S, *, tq, tk):
    nq = S // tq
    G, nh = 2, 4
    kern = functools.partial(_attn_kernel, mode="gqa", tq=tq, tk=tk, use_sel=True, nh=nh,
                             lambda_init=0.0)
    in_specs = [
        _smem_spec(),
        pl.BlockSpec((tq, 2 * LANES), lambda b, g, i: (b * nq + i, BLK_BQ // 2 + g)),
        pl.BlockSpec((S, LANES), lambda b, g, i: (b, BLK_SLC + g)),
        pl.BlockSpec((S, LANES), lambda b, g, i: (b, BLK_SLC + 2 + g)),
        pl.BlockSpec((S, LANES), lambda b, g, i: (0, 0)),
        pl.BlockSpec((tq, LANES), lambda b, g, i: (b * nq + i, g)),
    ]
    return pl.pallas_call(
        kern,
        grid=(B, G, nq),
        in_specs=in_specs,
        out_specs=pl.BlockSpec((tq, 2 * LANES), lambda b, g, i: (b * nq + i, g)),
        out_shape=jax.ShapeDtypeStruct((B * S, BRANCH_WIDTH), F32),
        scratch_shapes=_attn_scratch(tk // tq, nh * tq, tk, LANES),
        compiler_params=_cparams(("parallel", "parallel", "arbitrary")),
        name="nsa_slc_attn",
    )(slopes, proj, proj, proj, _key_extras(S, tk), sel)


def _cmp_attn_kernel(slopes_ref, q_ref, kc_ref, vc_ref, o_ref, sel_ref, *, tq, n_cmp, n_blk, n_sel):
    g = pl.program_id(1)
    i = pl.program_id(2)
    nh = NSA_HEADS // NSA_KV_GROUPS
    R = nh * tq
    qs = _stack_heads([q_ref[:, p * LANES:(p + 1) * LANES] for p in range(nh // 2)])
    slope_col = _head_column([slopes_ref[g * nh + h] for h in _stacked_head_order(nh)], tq)

    t = i * tq + (lax.broadcasted_iota(jnp.int32, (R, LANES), 0) & (tq - 1))
    c = lax.broadcasted_iota(jnp.int32, (R, LANES), 1)
    dist = t - (c * CMP_STRIDE + (CMP_LEN - 1))
    valid = (dist >= 0) & (c < n_cmp)
    s = _dot_nt(qs, kc_ref[...])
    s = jnp.where(valid, s - slope_col * dist.astype(F32), NEG_INF)
    m = jnp.max(s, axis=-1, keepdims=True)
    e = jnp.exp(s - m)
    p_c = e / jnp.sum(e, axis=-1, keepdims=True)
    any_valid = (t[:, 0:1] >= CMP_LEN - 1).astype(F32)
    p_c = p_c * any_valid
    o = _dot(p_c.astype(BF16), vc_ref[...])
    lane = lax.broadcasted_iota(jnp.int32, (tq, LANES), 1)
    lo_half = lane < HEAD_DIM
    for p in range(nh // 2):
        pair = jnp.where(lo_half, o[p * tq:(p + 1) * tq, :],
                         o[(nh // 2 + p) * tq:(nh // 2 + p + 1) * tq, :])
        o_ref[:, p * LANES:(p + 1) * LANES] = pair.astype(o_ref.dtype)

    p_sum = p_c[0:tq, :]
    for h in range(1, nh):
        p_sum = p_sum + p_c[h * tq:(h + 1) * tq, :]
    jj = lax.broadcasted_iota(jnp.int32, (n_blk, LANES), 0)
    cc = lax.broadcasted_iota(jnp.int32, (n_blk, LANES), 1)
    overlap = (jnp.minimum(cc * CMP_STRIDE + CMP_LEN, jj * SLC_BLOCK + SLC_BLOCK)
               - jnp.maximum(cc * CMP_STRIDE, jj * SLC_BLOCK))
    overlap = jnp.where(cc < n_cmp, jnp.maximum(overlap, 0), 0)
    w_t = (overlap.astype(F32) * (1.0 / CMP_LEN)).astype(BF16)
    hi = p_sum.astype(BF16)
    rem = p_sum - hi.astype(F32)
    mid = rem.astype(BF16)
    low = (rem - mid.astype(F32)).astype(BF16)
    score = _dot_nt(w_t, hi) + _dot_nt(w_t, mid) + _dot_nt(w_t, low)

    tpos = i * tq + lax.broadcasted_iota(jnp.int32, (n_blk, tq), 1)
    t_blk = tpos >> _log2(SLC_BLOCK)
    jb = lax.broadcasted_iota(jnp.int32, (n_blk, tq), 0)
    score = jnp.where((jb == 0) | (jb == t_blk) | (jb == t_blk - 1), FORCED_SCORE, score)
    score = jnp.where(jb > t_blk, -1.0, score)
    rank = jnp.zeros((n_blk, tq), F32)
    for kk in range(n_blk):
        row = score[kk:kk + 1, :]
        ahead = (row > score) | ((row == score) & (kk < jb))
        rank = rank + ahead.astype(F32)
    sel_t = (rank < n_sel).astype(BF16)
    sel_t = jnp.concatenate([sel_t, jnp.zeros((LANES - n_blk, tq), BF16)], axis=0)
    eye = (lax.broadcasted_iota(jnp.int32, (tq, tq), 0)
           == lax.broadcasted_iota(jnp.int32, (tq, tq), 1)).astype(BF16)
    sel_ref[...] = _dot_nt(eye, sel_t).astype(sel_ref.dtype)


def _cmp_attn(proj, kvc, slopes, B, S, *, tq=512):
    nq = S // tq
    G = NSA_KV_GROUPS
    n_cmp = (S - CMP_LEN) // CMP_STRIDE + 1
    n_blk = S // SLC_BLOCK
    n_sel = min(SLC_TOPK, n_blk)
    kern = functools.partial(_cmp_attn_kernel, tq=tq, n_cmp=n_cmp, n_blk=n_blk, n_sel=n_sel)
    return pl.pallas_call(
        kern,
        grid=(B, G, nq),
        in_specs=[_smem_spec(),
                  pl.BlockSpec((tq, 2 * LANES), lambda b, g, i: (b * nq + i, BLK_BQ // 2 + g)),
                  pl.BlockSpec((None, LANES, LANES), lambda b, g, i: (0, b * G + g, 0)),
                  pl.BlockSpec((None, LANES, LANES), lambda b, g, i: (1, b * G + g, 0))],
        out_specs=[pl.BlockSpec((tq, 2 * LANES), lambda b, g, i: (b * nq + i, g)),
                   pl.BlockSpec((tq, LANES), lambda b, g, i: (b * nq + i, g))],
        out_shape=[jax.ShapeDtypeStruct((B * S, BRANCH_WIDTH), F32),
                   jax.ShapeDtypeStruct((B * S, G * LANES), BF16)],
        compiler_params=_cparams(("parallel", "parallel", "parallel")),
        name="nsa_cmp_attn",
    )(slopes, proj, kvc, kvc)


def _interleave_matrix(tm, n_outer):
    e_cnt = tm // n_outer
    r_out = lax.broadcasted_iota(jnp.int32, (tm, tm), 0)
    r_in = lax.broadcasted_iota(jnp.int32, (tm, tm), 1)
    outer = r_in >> _log2(e_cnt)
    inner = r_in & (e_cnt - 1)
    return (r_out == inner * n_outer + outer).astype(BF16)


def _permute_rows(pm, x):
    if x.dtype == BF16:
        return _dot(pm, x)
    hi = x.astype(BF16)
    rem = x - hi.astype(F32)
    mid = rem.astype(BF16)
    low = (rem - mid.astype(F32)).astype(BF16)
    return _dot(pm, hi) + _dot(pm, mid) + _dot(pm, low)


def _merge_kernel(x_ref, oa_ref, ocmp_ref, oslc_ref, owin_ref, oc_ref, gts_ref,
                  wmg_ref, wbr_ref, wout_ref, g_ref, b_ref, o_ref):
    tm = x_ref.shape[0]
    x = x_ref[...]
    xb = x.astype(BF16)
    p_slc = _interleave_matrix(tm, oslc_ref.shape[0])
    p_band = _interleave_matrix(tm, owin_ref.shape[0])
    o_slc = _permute_rows(p_slc, oslc_ref[...].reshape(tm, BRANCH_WIDTH))
    o_win = _permute_rows(p_band, owin_ref[...].reshape(tm, BRANCH_WIDTH))
    o_c = _permute_rows(p_band, oc_ref[...].reshape(tm, BRANCH_WIDTH)).astype(BF16)
    gates = jax.nn.sigmoid(gts_ref[...])
    lane = lax.broadcasted_iota(jnp.int32, (tm, LANES), 1)
    lo_half = lane < HEAD_DIM
    nsa = (ocmp_ref[...], o_slc, o_win)
    ob_tiles = []
    for pair in range(NSA_HEADS // 2):
        acc = jnp.zeros((tm, LANES), F32)
        for n in range(3):
            c0 = n * NSA_HEADS + 2 * pair
            gate = jnp.where(lo_half, gates[:, c0:c0 + 1], gates[:, c0 + 1:c0 + 2])
            acc = acc + gate * nsa[n][:, pair * LANES:(pair + 1) * LANES]
        ob_tiles.append(acc.astype(BF16))
    ob = jnp.concatenate(ob_tiles, axis=1)
    branches = (oa_ref[...], ob, o_c)
    y = jnp.zeros((tm, D_MODEL), F32)
    for n in range(N_BRANCHES):
        mg = jax.nn.sigmoid(_dot(xb, wmg_ref[:, n * D_MODEL:(n + 1) * D_MODEL]))
        y = y + mg * _dot(branches[n], wbr_ref[n])
    mix = _dot(y.astype(BF16), wout_ref[...])
    o_ref[...] = _layer_norm(ALPHA * x + mix, g_ref[...], b_ref[...])


def _merge(x, o_a, o_cmp, o_slc, o_win, o_c, projf, w_mg, w_br, w_out, ln_g, ln_b, B, S,
           *, tm=512):
    T, D = x.shape
    nt = S // tm
    n_slc, n_band = S // SLC_QUERY_CHUNK, S // QUERY_BLOCK
    o_slc = o_slc.reshape(B, n_slc, SLC_QUERY_CHUNK, BRANCH_WIDTH)
    o_win = o_win.reshape(B, n_band, QUERY_BLOCK, BRANCH_WIDTH)
    o_c = o_c.reshape(B, n_band, QUERY_BLOCK, BRANCH_WIDTH)
    row = lambda b, a: (b * nt + a, 0)
    const2 = lambda b, a: (0, 0)
    chunked = lambda b, a: (b, 0, a, 0)
    return pl.pallas_call(
        _merge_kernel,
        grid=(B, nt),
        in_specs=[pl.BlockSpec((tm, D), row),
                  pl.BlockSpec((tm, BRANCH_WIDTH), row),
                  pl.BlockSpec((tm, BRANCH_WIDTH), row),
                  pl.BlockSpec((None, n_slc, tm // n_slc, BRANCH_WIDTH), chunked),
                  pl.BlockSpec((None, n_band, tm // n_band, BRANCH_WIDTH), chunked),
                  pl.BlockSpec((None, n_band, tm // n_band, BRANCH_WIDTH), chunked),
                  pl.BlockSpec((tm, LANES), lambda b, a: (b * nt + a, 2)),
                  pl.BlockSpec((D, N_BRANCHES * D), const2),
                  pl.BlockSpec((N_BRANCHES, BRANCH_WIDTH, D), lambda b, a: (0, 0, 0)),
                  pl.BlockSpec((D, D), const2),
                  pl.BlockSpec((1, D), const2),
                  pl.BlockSpec((1, D), const2)],
        out_specs=pl.BlockSpec((tm, D), row),
        out_shape=jax.ShapeDtypeStruct((T, D), F32),
        compiler_params=_cparams(("parallel", "parallel")),
        name="merge",
    )(x, o_a, o_cmp, o_slc, o_win, o_c, projf, w_mg, w_br, w_out, ln_g, ln_b)


def _dup_kv(w, groups):
    D = w.shape[0]
    w = w.reshape(D, 2 * groups, 1, HEAD_DIM)
    return jnp.broadcast_to(w, (D, 2 * groups, 2, HEAD_DIM)).reshape(D, 4 * groups * HEAD_DIM)


def _split_w_in(w):
    sizes = (512, 512, 512, 512, 256, 256, 256, 24, 512, 128, 128, 3072)
    offs = [0]
    for n in sizes:
        offs.append(offs[-1] + n)
    (a_q, a_k, a_v, b_q, b_kvc, b_kvs, b_kvw, b_g, c_q, c_k, c_v, m_g) = [
        w[:, offs[n]:offs[n + 1]] for n in range(len(sizes))]
    w_bf = jnp.concatenate(
        [a_q, a_k, a_v, b_q, c_q, _dup_kv(b_kvs, 2), _dup_kv(b_kvw, 2),
         _dup_kv(jnp.concatenate([c_k, c_v], axis=1), 2)], axis=1).astype(BF16)
    pad = jnp.zeros((w.shape[0], N_PROJ_F32 - 256 - 24), w.dtype)
    w_f32 = jnp.concatenate([b_kvc, b_g, pad], axis=1).astype(BF16)
    return w_bf, w_f32, m_g.astype(BF16)


def _alibi_slopes(n_heads):
    assert 8 % n_heads == 0
    return jnp.exp2(-8.0 * jnp.arange(1, n_heads + 1, dtype=F32) / n_heads)


def _token_mixing(x, B, S, w_in, diff_lam, diff_subln_g, cmp_pos, cmp_w1, cmp_w2, sinks,
                  w_branch, w_out, ln_g, ln_b, lambda_init):
    G = NSA_KV_GROUPS
    w_bf, w_f32, w_mg = _split_w_in(w_in)
    proj, projf = _proj(x, w_bf, w_f32)

    n_rows = S // CMP_STRIDE
    r = projf[:, :4 * HEAD_DIM].reshape(B, S, 2, G, HEAD_DIM).transpose(2, 0, 3, 1, 4)
    r = r.reshape(2, B * G * n_rows, CMP_STRIDE * HEAD_DIM)
    pos = cmp_pos.reshape(2, 2, CMP_STRIDE * HEAD_DIM)
    w2dup = jnp.concatenate([cmp_w2, cmp_w2], axis=-1).astype(BF16)
    kvc = _compress(r, pos, cmp_w1.astype(BF16), w2dup)

    slopes8 = _alibi_slopes(NSA_HEADS)
    o_a = _diff_attn(proj, _alibi_slopes(DIFF_HEADS), diff_lam, diff_subln_g.reshape(1, LANES),
                     B, S, lambda_init, tq=512, tk=512)
    o_cmp, sel = _cmp_attn(proj, kvc, slopes8, B, S)
    o_slc = _slc_attn(proj, slopes8, sel, B, S, tq=512, tk=512)
    o_win = _window_attn(proj, slopes8, B, S, q_blk=BLK_BQ, kv_blk=BLK_WIN, window=NSA_WINDOW,
                         out_dtype=BF16, tq=256, name="nsa_win_attn")
    o_c = _window_attn(proj, _alibi_slopes(SWA_HEADS), B, S, q_blk=BLK_CQ, kv_blk=BLK_SWA,
                       window=SWA_WINDOW, sinks=sinks, out_dtype=BF16, tq=512, name="swa_attn")
    return _merge(x, o_a, o_cmp, o_slc, o_win, o_c, projf, w_mg, w_branch.astype(BF16),
                  w_out.astype(BF16), ln_g, ln_b, B, S)


def kernel(x, p, ffn_w_in, ffn_w_out, ln_g, ln_b, w_in, diff_lam, diff_subln_g, nsa_cmp_pos,
           nsa_cmp_w1, nsa_cmp_w2, swa_sinks, w_branch, w_out, ple_w_in, ple_w_gate):
    B, S, D = x.shape
    T = B * S
    h = x.reshape(T, D)
    for i in range(DEPTH):
        lambda_init = 0.8 - 0.6 * math.exp(-0.3 * i)
        lg = ln_g[i].reshape(3, 1, D)
        lb = ln_b[i].reshape(3, 1, D)
        h = _ffn(h, ffn_w_in[i, 0].astype(BF16), ffn_w_out[i, 0].astype(BF16), lg[0], lb[0])
        h = _token_mixing(h, B, S, w_in[i], diff_lam[i], diff_subln_g[i], nsa_cmp_pos[i],
                          nsa_cmp_w1[i], nsa_cmp_w2[i], swa_sinks[i], w_branch[i], w_out[i],
                          lg[1], lb[1], lambda_init)
        h = _ffn(h, ffn_w_in[i, 1].astype(BF16), ffn_w_out[i, 1].astype(BF16), lg[2], lb[2],
                 ple_args=(p[i].reshape(T, PLE_DIM), ple_w_gate[i].astype(BF16),
                           ple_w_in[i].astype(BF16)))
    return h.reshape(B, S, D)
```

```python
import functools
import math

import jax
import jax.numpy as jnp
from jax import lax
from jax.experimental import pallas as pl
from jax.experimental.pallas import tpu as pltpu

F32 = jnp.float32
BF16 = jnp.bfloat16

D_MODEL = 1024
DEPTH = 2
HEAD_DIM = 64
DIFF_HEADS = 4
NSA_HEADS = 8
NSA_KV_GROUPS = 2
CMP_LEN = 32
CMP_STRIDE = 16
CMP_HIDDEN = 256
SLC_BLOCK = 64
SLC_TOPK = 16
NSA_WINDOW = 512
FORCED_SCORE = 1e4
SWA_HEADS = 8
SWA_KV_HEADS = 2
SWA_WINDOW = 128
QUERY_BLOCK = 128
SLC_QUERY_CHUNK = 32
N_BRANCHES = 3
BRANCH_WIDTH = 512
D_FF = 2816
PLE_DIM = 256
LN_EPS = 1e-5
SUBLN_EPS = 1e-5
ALPHA = (2 * DEPTH) ** 0.25
NEG_INF = -1e30
QK_SCALE = HEAD_DIM ** -0.5

LANES = 128
VMEM_LIMIT = 56 * 1024 * 1024

BLK_AQ, BLK_AK, BLK_AV = 0, 4, 8
BLK_BQ, BLK_CQ = 12, 16
BLK_SLC, BLK_WIN, BLK_SWA = 20, 24, 28
N_PROJ_BF = 32 * LANES
N_PROJ_F32 = 3 * LANES


def _cparams(sem):
    return pltpu.CompilerParams(dimension_semantics=sem, vmem_limit_bytes=VMEM_LIMIT)


def _layer_norm(h, g, b):
    mu = jnp.mean(h, axis=-1, keepdims=True)
    d = h - mu
    var = jnp.mean(d * d, axis=-1, keepdims=True)
    return d * lax.rsqrt(var + LN_EPS) * g + b


def _dot(a, b):
    return jnp.dot(a, b, preferred_element_type=F32)


def _dot_nt(a, b):
    return lax.dot_general(a, b, (((1,), (1,)), ((), ())), preferred_element_type=F32)


FF_CHUNK = 256
EPILOGUE_ROWS = 256


def _ffn_kernel(*refs, ple):
    if ple:
        x_ref, win_ref, wout_ref, g_ref, b_ref, p_ref, pwg_ref, pwi_ref, o_ref, acc_ref = refs
    else:
        x_ref, win_ref, wout_ref, g_ref, b_ref, o_ref, acc_ref = refs
    xb = x_ref[...].astype(BF16)
    for k in range(D_FF // FF_CHUNK):
        c0 = k * FF_CHUNK
        gate = _dot(xb, win_ref[:, c0:c0 + FF_CHUNK])
        up = _dot(xb, win_ref[:, D_FF + c0:D_FF + c0 + FF_CHUNK])
        hid = ((gate * jax.nn.sigmoid(gate)) * up).astype(BF16)
        contrib = _dot(hid, wout_ref[c0:c0 + FF_CHUNK, :])
        if k == 0:
            acc_ref[...] = contrib
        else:
            acc_ref[...] += contrib
    for r0 in range(0, x_ref.shape[0], EPILOGUE_ROWS):
        rows = slice(r0, r0 + EPILOGUE_ROWS)
        h = ALPHA * x_ref[rows, :] + 0.5 * acc_ref[rows, :]
        if ple:
            gp = jax.nn.sigmoid(_dot(h.astype(BF16), pwg_ref[...]))
            h = h + gp * _dot(p_ref[rows, :].astype(BF16), pwi_ref[...])
        o_ref[rows, :] = _layer_norm(h, g_ref[...], b_ref[...])


def _resident(shape):
    return pl.BlockSpec(shape, lambda i: (0,) * len(shape), pipeline_mode=pl.Buffered(1))


def _ffn(x, w_in, w_out, ln_g, ln_b, ple_args=None, *, tm=512):
    T, D = x.shape
    ple = ple_args is not None
    in_specs = [pl.BlockSpec((tm, D), lambda i: (i, 0)),
                _resident(w_in.shape), _resident(w_out.shape),
                _resident((1, D)), _resident((1, D))]
    args = [x, w_in, w_out, ln_g, ln_b]
    if ple:
        p, pwg, pwi = ple_args
        in_specs += [pl.BlockSpec((tm, PLE_DIM), lambda i: (i, 0)),
                     _resident(pwg.shape), _resident(pwi.shape)]
        args += [p, pwg, pwi]
    return pl.pallas_call(
        functools.partial(_ffn_kernel, ple=ple),
        grid=(T // tm,),
        in_specs=in_specs,
        out_specs=pl.BlockSpec((tm, D), lambda i: (i, 0)),
        out_shape=jax.ShapeDtypeStruct((T, D), F32),
        scratch_shapes=[pltpu.VMEM((tm, D), F32)],
        compiler_params=_cparams(("parallel",)),
        name="ffn_ple" if ple else "ffn",
    )(*args)


PROJ_COLS = 1024


def _proj_kernel(x_ref, wb_ref, wf_ref, ob_ref, of_ref):
    xb = x_ref[...].astype(BF16)
    for n0 in range(0, wb_ref.shape[1], PROJ_COLS):
        ob_ref[:, n0:n0 + PROJ_COLS] = _dot(xb, wb_ref[:, n0:n0 + PROJ_COLS]).astype(BF16)
    of_ref[...] = _dot(xb, wf_ref[...])


def _proj(x, w_bf, w_f32, *, tm=512):
    T, K = x.shape
    nb, nf = w_bf.shape[1], w_f32.shape[1]
    return pl.pallas_call(
        _proj_kernel,
        grid=(T // tm,),
        in_specs=[pl.BlockSpec((tm, K), lambda i: (i, 0)),
                  pl.BlockSpec((K, nb), lambda i: (0, 0)),
                  pl.BlockSpec((K, nf), lambda i: (0, 0))],
        out_specs=[pl.BlockSpec((tm, nb), lambda i: (i, 0)),
                   pl.BlockSpec((tm, nf), lambda i: (i, 0))],
        out_shape=[jax.ShapeDtypeStruct((T, nb), BF16), jax.ShapeDtypeStruct((T, nf), F32)],
        compiler_params=_cparams(("parallel",)),
        name="proj",
    )(x, w_bf, w_f32)


def _compress_kernel(r_ref, pos_ref, w1_ref, w2_ref, o_ref):
    r = r_ref[...]
    half = CMP_STRIDE * HEAD_DIM
    lo = _dot((r + pos_ref[0:1, :]).astype(BF16), w1_ref[0:half, :])
    hi = _dot((r + pos_ref[1:2, :]).astype(BF16), w1_ref[half:2 * half, :])
    pre = lo + pltpu.roll(hi, hi.shape[0] - 1, 0)
    hid = jax.nn.gelu(pre, approximate=True)
    o_ref[...] = _dot(hid.astype(BF16), w2_ref[...]).astype(o_ref.dtype)


def _compress(r, pos, w1, w2dup, *, tm=512):
    _, M, F = r.shape
    tm = min(tm, M)
    return pl.pallas_call(
        _compress_kernel,
        grid=(2, M // tm),
        in_specs=[pl.BlockSpec((None, tm, F), lambda n, i: (n, i, 0)),
                  pl.BlockSpec((None, 2, F), lambda n, i: (n, 0, 0)),
                  pl.BlockSpec((None, 2 * F, CMP_HIDDEN), lambda n, i: (n, 0, 0)),
                  pl.BlockSpec((None, CMP_HIDDEN, LANES), lambda n, i: (n, 0, 0))],
        out_specs=pl.BlockSpec((None, tm, LANES), lambda n, i: (n, i, 0)),
        out_shape=jax.ShapeDtypeStruct((2, M, LANES), BF16),
        compiler_params=_cparams(("parallel", "parallel")),
        name="nsa_compress",
    )(r, pos, w1, w2dup)


def _log2(n):
    assert n > 0 and n & (n - 1) == 0, n
    return n.bit_length() - 1


def _stack_heads(q_tiles):
    tq = q_tiles[0].shape[0]
    lane = lax.broadcasted_iota(jnp.int32, (tq, LANES), 1)
    lo = lane < HEAD_DIM
    zero = jnp.zeros((tq, LANES), BF16)
    scaled = [q * jnp.asarray(QK_SCALE, BF16) for q in q_tiles]
    parts = [jnp.where(lo, qs, zero) for qs in scaled] + [jnp.where(lo, zero, qs) for qs in scaled]
    return jnp.concatenate(parts, axis=0)


def _stacked_head_order(nh):
    return list(range(0, nh, 2)) + list(range(1, nh, 2))


def _head_column(values, tq):
    nh = len(values)
    head = lax.broadcasted_iota(jnp.int32, (nh * tq, 1), 0) >> _log2(tq)
    col = jnp.zeros((nh * tq, 1), F32)
    for h in range(nh):
        col = jnp.where(head == h, values[h], col)
    return col


KX_ONE_A, KX_COL_HI, KX_COL_LO, KX_ONE_B = 32, 33, 34, 35
POS_SPLIT = 16


def _key_extras(S, tk):
    assert S // SLC_BLOCK <= KX_ONE_A
    key = jnp.arange(S, dtype=jnp.int32)[:, None]
    lane = jnp.arange(LANES, dtype=jnp.int32)[None, :]
    col = key % tk
    x = jnp.where(lane == key // SLC_BLOCK, -NEG_INF, 0.0)
    x = jnp.where((lane == KX_ONE_A) | (lane == KX_ONE_B), 1.0, x)
    x = jnp.where(lane == KX_COL_HI, (col - col % POS_SPLIT).astype(F32), x)
    x = jnp.where(lane == KX_COL_LO, (col % POS_SPLIT).astype(F32), x)
    return x.astype(BF16)


SOFTMAX_ROWS = 32


def _attn_kernel(*refs, mode, tq, tk, use_sel, nh, lambda_init):
    refs = list(refs)
    slopes_ref = refs.pop(0)
    q_ref, k_ref, v_ref, kx_ref = refs.pop(0), refs.pop(0), refs.pop(0), refs.pop(0)
    sel_ref = refs.pop(0) if use_sel else None
    if mode == "diff":
        lam_ref, subg_ref = refs.pop(0), refs.pop(0)
    o_ref = refs.pop(0)
    mask_ref, qs_ref, s_ref, p_ref, m_ref, acc_ref, shift_ref = refs
    hg = pl.program_id(1)
    i = pl.program_id(2)
    R = nh * tq
    gqa = mode == "gqa"
    ratio = tk // tq
    half = R // 2

    xl = lax.broadcasted_iota(jnp.int32, (R, LANES), 1)

    @pl.when(i == 0)
    def _():
        if gqa:
            slope_col = _head_column([slopes_ref[hg * nh + h] for h in _stacked_head_order(nh)], tq)
        else:
            slope_col = jnp.full((R, 1), slopes_ref[hg], F32)
        shift_ref[...] = slope_col * float(tk)
        row = lax.broadcasted_iota(jnp.int32, (R, tk), 0) & (tq - 1)
        col = lax.broadcasted_iota(jnp.int32, (R, tk), 1)
        for a in range(ratio):
            mask_ref[a] = jnp.where(row - col + a * tq >= 0, 0.0, NEG_INF)
        xr = lax.broadcasted_iota(jnp.int32, (R, LANES), 0) & (tq - 1)
        xr_lo = xr & (POS_SPLIT - 1)
        qx = jnp.where(xl == KX_ONE_A, -slope_col * (xr - xr_lo).astype(F32), 0.0)
        qx = jnp.where(xl == KX_ONE_B, -slope_col * xr_lo.astype(F32), qx)
        qx = jnp.where((xl == KX_COL_HI) | (xl == KX_COL_LO), slope_col, qx)
        qs_ref[:, LANES:2 * LANES] = qx.astype(BF16)

    if gqa:
        qs_ref[:, 0:LANES] = _stack_heads([q_ref[:, p * LANES:(p + 1) * LANES]
                                           for p in range(nh // 2)])
    else:
        qs_ref[:, 0:LANES] = _stack_heads([q_ref[...]])
    if use_sel:
        n_blk = k_ref.shape[0] // SLC_BLOCK
        not_sel = jnp.concatenate([(sel_ref[...].astype(F32) - 1.0).astype(BF16)] * nh, axis=0)
        qs_ref[:, LANES:2 * LANES] = jnp.where(xl < n_blk, not_sel, qs_ref[:, LANES:2 * LANES])
    m_ref[...] = jnp.full((R, 1), NEG_INF, F32)
    acc_ref[...] = jnp.zeros(acc_ref.shape, F32)
    lo_half_k = lax.broadcasted_iota(jnp.int32, (tk, LANES), 1) < HEAD_DIM

    def scores(j, slot):
        start = pl.multiple_of(j * tk, tk)
        kk = jnp.concatenate([k_ref[pl.ds(start, tk), :], kx_ref[pl.ds(start, tk), :]], axis=1)
        for h0 in (0, half):
            s_ref[slot, h0:h0 + half, :] = _dot_nt(qs_ref[h0:h0 + half, :], kk)

    def softmax_pv(j, slot, mask_idx):
        start = pl.multiple_of(j * tk, tk)
        v = v_ref[pl.ds(start, tk), :]
        one = jnp.ones((tk, LANES), BF16)
        if gqa:
            v_halves = (jnp.where(lo_half_k, v, one), jnp.where(lo_half_k, one, v))
        else:
            v_halves = (jnp.concatenate([v, one], axis=1),) * 2
        for hh, h0 in enumerate((0, half)):
            for c in range(half // SOFTMAX_ROWS):
                r0 = h0 + c * SOFTMAX_ROWS
                rows = slice(r0, r0 + SOFTMAX_ROWS)
                s = s_ref[slot, rows, :]
                if mask_idx is not None:
                    s = s + mask_ref[mask_idx, rows, :]
                m_prev = m_ref[rows, :] - shift_ref[rows, :]
                m_new = jnp.maximum(m_prev, jnp.max(s, axis=-1, keepdims=True))
                alpha = jnp.exp(m_prev - m_new)
                p_ref[rows, :] = jnp.exp(s - m_new).astype(BF16)
                m_ref[rows, :] = m_new
                acc_ref[rows, :] = alpha * acc_ref[rows, :]
            acc_ref[h0:h0 + half, :] += _dot(p_ref[h0:h0 + half, :], v_halves[hh])

    j_last = i // ratio
    scores(0, 0)

    def pair(t, c):
        scores(2 * t + 1, 1)
        softmax_pv(2 * t, 0, None)
        scores(2 * t + 2, 0)
        softmax_pv(2 * t + 1, 1, None)
        return c

    lax.fori_loop(0, j_last // 2, pair, 0)

    @pl.when(j_last % 2 == 1)
    def _():
        scores(j_last, 1)
        softmax_pv(j_last - 1, 0, None)
        softmax_pv(j_last, 1, i % ratio)

    @pl.when(j_last % 2 == 0)
    def _():
        softmax_pv(j_last, 0, i % ratio)

    acc = acc_ref[...]
    if gqa:
        acc_e, acc_o = acc[:half], acc[half:]
        l_e = acc_e[:, HEAD_DIM:HEAD_DIM + 1]
        l_o = acc_o[:, 0:1]
        o_e = acc_e / l_e
        o_o = acc_o / l_o
        lo_half_q = lax.broadcasted_iota(jnp.int32, (tq, LANES), 1) < HEAD_DIM
        for p_ in range(nh // 2):
            pair = jnp.where(lo_half_q, o_e[p_ * tq:(p_ + 1) * tq, :], o_o[p_ * tq:(p_ + 1) * tq, :])
            o_ref[:, p_ * LANES:(p_ + 1) * LANES] = pair.astype(o_ref.dtype)
    else:
        o = acc[:, 0:LANES] / acc[:, LANES:LANES + 1]
        lam = lam_ref[...]
        lam_full = (jnp.exp(jnp.sum(lam[0:1, :] * lam[1:2, :], axis=-1, keepdims=True))
                    - jnp.exp(jnp.sum(lam[2:3, :] * lam[3:4, :], axis=-1, keepdims=True))
                    + lambda_init)
        od = o[0:tq, :] - lam_full * o[tq:2 * tq, :]
        od = od * lax.rsqrt(jnp.mean(od * od, axis=-1, keepdims=True) + SUBLN_EPS) * subg_ref[...]
        o_ref[...] = (od * (1.0 - lambda_init)).astype(o_ref.dtype)


def _attn_scratch(n_masks, R, tk, acc_width):
    return [pltpu.VMEM((n_masks, R, tk), F32),
            pltpu.VMEM((R, 2 * LANES), BF16),
            pltpu.VMEM((2, R, tk), F32),
            pltpu.VMEM((R, tk), BF16),
            pltpu.VMEM((R, 1), F32),
            pltpu.VMEM((R, acc_width), F32),
            pltpu.VMEM((R, 1), F32)]


def _window_kernel(*refs, tq, tsub, back, window, use_sink, nh, n_special):
    refs = list(refs)
    slopes_ref = refs.pop(0)
    sinks_ref = refs.pop(0) if use_sink else None
    q_ref, k_ref, v_ref, o_ref, bias_ref, s_ref, p_ref, m_ref, sink_ref = refs
    g = pl.program_id(1)
    i = pl.program_id(2)
    R = nh * tsub
    half = R // 2
    span = back + tsub
    n_sub = tq // tsub
    order = _stacked_head_order(nh)

    @pl.when(i == 0)
    def _():
        slope_col = _head_column([slopes_ref[g * nh + h] for h in order], tsub)
        if use_sink:
            sink_ref[...] = _head_column([sinks_ref[g * nh + h] for h in order], tsub)
        row = lax.broadcasted_iota(jnp.int32, (R, span), 0) & (tsub - 1)
        col = lax.broadcasted_iota(jnp.int32, (R, span), 1)
        for a in range(n_special + 1):
            dist = row - col + (a * tsub if a < n_special else back)
            ok = (dist >= 0) & (dist < window)
            bias_ref[a] = jnp.where(ok, -slope_col * dist.astype(F32), NEG_INF)

    lo_half_k = lax.broadcasted_iota(jnp.int32, (span, LANES), 1) < HEAD_DIM
    lo_half_q = lax.broadcasted_iota(jnp.int32, (tsub, LANES), 1) < HEAD_DIM
    one = jnp.ones((span, LANES), BF16)
    for sub in range(n_sub):
        it = i * n_sub + sub
        a = jnp.minimum(it, n_special)
        start = pl.multiple_of(jnp.maximum(it * tsub - back, 0), LANES)
        q_rows = slice(sub * tsub, (sub + 1) * tsub)
        qs = _stack_heads([q_ref[q_rows, p * LANES:(p + 1) * LANES] for p in range(nh // 2)])
        k = k_ref[pl.ds(start, span), :]
        v = v_ref[pl.ds(start, span), :]
        v_halves = (jnp.where(lo_half_k, v, one), jnp.where(lo_half_k, one, v))
        for h0 in (0, half):
            s_ref[sub, h0:h0 + half, :] = _dot_nt(qs[h0:h0 + half, :], k)
        accs = []
        for hh, h0 in enumerate((0, half)):
            for c in range(half // SOFTMAX_ROWS):
                r0 = h0 + c * SOFTMAX_ROWS
                rows = slice(r0, r0 + SOFTMAX_ROWS)
                s = s_ref[sub, rows, :] + bias_ref[a, rows, :]
                m = jnp.max(s, axis=-1, keepdims=True)
                p_ref[sub, rows, :] = jnp.exp(s - m).astype(BF16)
                m_ref[sub, rows, :] = m
            accs.append(_dot(p_ref[sub, h0:h0 + half, :], v_halves[hh]))

        acc_e, acc_o = accs
        l_e = acc_e[:, HEAD_DIM:HEAD_DIM + 1]
        l_o = acc_o[:, 0:1]
        if use_sink:
            extra = jnp.exp(sink_ref[...] - m_ref[sub])
            l_e = l_e + extra[:half]
            l_o = l_o + extra[half:]
        o_e = acc_e / l_e
        o_o = acc_o / l_o
        for p_ in range(nh // 2):
            pair = jnp.where(lo_half_q, o_e[p_ * tsub:(p_ + 1) * tsub, :],
                             o_o[p_ * tsub:(p_ + 1) * tsub, :])
            o_ref[q_rows, p_ * LANES:(p_ + 1) * LANES] = pair.astype(o_ref.dtype)


def _window_attn(proj, slopes, B, S, *, q_blk, kv_blk, window, sinks=None, out_dtype, tq, tsub,
                 name):
    nq = S // tq
    G, nh = 2, 4
    back = -(-window // LANES) * LANES
    n_special = -(-back // tsub)
    span = back + tsub
    n_sub = tq // tsub
    R = nh * tsub
    assert span <= S
    kern = functools.partial(_window_kernel, tq=tq, tsub=tsub, back=back, window=window,
                             use_sink=sinks is not None, nh=nh, n_special=n_special)
    in_specs = [_smem_spec()]
    args = [slopes]
    if sinks is not None:
        in_specs.append(_smem_spec())
        args.append(sinks)
    in_specs += [
        pl.BlockSpec((tq, 2 * LANES), lambda b, g, i: (b * nq + i, q_blk // 2 + g)),
        pl.BlockSpec((S, LANES), lambda b, g, i: (b, kv_blk + g)),
        pl.BlockSpec((S, LANES), lambda b, g, i: (b, kv_blk + 2 + g)),
    ]
    args += [proj, proj, proj]
    return pl.pallas_call(
        kern,
        grid=(B, G, nq),
        in_specs=in_specs,
        out_specs=pl.BlockSpec((tq, 2 * LANES), lambda b, g, i: (b * nq + i, g)),
        out_shape=jax.ShapeDtypeStruct((B * S, BRANCH_WIDTH), out_dtype),
        scratch_shapes=[pltpu.VMEM((n_special + 1, R, span), F32),
                        pltpu.VMEM((n_sub, R, span), F32),
                        pltpu.VMEM((n_sub, R, span), BF16),
                        pltpu.VMEM((n_sub, R, 1), F32),
                        pltpu.VMEM((R, 1), F32)],
        compiler_params=_cparams(("parallel", "parallel", "arbitrary")),
        name=name,
    )(*args)


def _smem_spec():
    return pl.BlockSpec(memory_space=pltpu.SMEM)


def _diff_attn(proj, slopes, lam, subg, B, S, lambda_init, *, tq, tk):
    nq = S // tq
    kern = functools.partial(_attn_kernel, mode="diff", tq=tq, tk=tk, use_sel=False, nh=2,
                             lambda_init=lambda_init)
    return pl.pallas_call(
        kern,
        grid=(B, DIFF_HEADS, nq),
        in_specs=[_smem_spec(),
                  pl.BlockSpec((tq, LANES), lambda b, h, i: (b * nq + i, BLK_AQ + h)),
                  pl.BlockSpec((S, LANES), lambda b, h, i: (b, BLK_AK + h)),
                  pl.BlockSpec((S, LANES), lambda b, h, i: (b, BLK_AV + h)),
                  pl.BlockSpec((S, LANES), lambda b, h, i: (0, 0)),
                  pl.BlockSpec((4, HEAD_DIM), lambda b, h, i: (0, 0)),
                  pl.BlockSpec((1, LANES), lambda b, h, i: (0, 0))],
        out_specs=pl.BlockSpec((tq, LANES), lambda b, h, i: (b * nq + i, h)),
        out_shape=jax.ShapeDtypeStruct((B * S, BRANCH_WIDTH), BF16),
        scratch_shapes=_attn_scratch(tk // tq, 2 * tq, tk, 2 * LANES),
        compiler_params=_cparams(("parallel", "parallel", "arbitrary")),
        name="diff_attn",
    )(slopes, proj, proj, proj, _key_extras(S, tk), lam, subg)


def _slc_attn(proj, slopes, sel, B, S, *, tq, tk):
    nq = S // tq
    G, nh = 2, 4
    kern = functools.partial(_attn_kernel, mode="gqa", tq=tq, tk=tk, use_sel=True, nh=nh,
                             lambda_init=0.0)
    in_specs = [
        _smem_spec(),
        pl.BlockSpec((tq, 2 * LANES), lambda b, g, i: (b * nq + i, BLK_BQ // 2 + g)),
        pl.BlockSpec((S, LANES), lambda b, g, i: (b, BLK_SLC + g)),
        pl.BlockSpec((S, LANES), lambda b, g, i: (b, BLK_SLC + 2 + g)),
        pl.BlockSpec((S, LANES), lambda b, g, i: (0, 0)),
        pl.BlockSpec((tq, LANES), lambda b, g, i: (b * nq + i, g)),
    ]
    return pl.pallas_call(
        kern,
        grid=(B, G, nq),
        in_specs=in_specs,
        out_specs=pl.BlockSpec((tq, 2 * LANES), lambda b, g, i: (b * nq + i, g)),
        out_shape=jax.ShapeDtypeStruct((B * S, BRANCH_WIDTH), F32),
        scratch_shapes=_attn_scratch(tk // tq, nh * tq, tk, LANES),
        compiler_params=_cparams(("parallel", "parallel", "arbitrary")),
        name="nsa_slc_attn",
    )(slopes, proj, proj, proj, _key_extras(S, tk), sel)


def _cmp_attn_kernel(slopes_ref, q_ref, kc_ref, vc_ref, o_ref, sel_ref, *, tq, n_cmp, n_blk, n_sel):
    g = pl.program_id(1)
    i = pl.program_id(2)
    nh = NSA_HEADS // NSA_KV_GROUPS
    R = nh * tq
    qs = _stack_heads([q_ref[:, p * LANES:(p + 1) * LANES] for p in range(nh // 2)])
    slope_col = _head_column([slopes_ref[g * nh + h] for h in _stacked_head_order(nh)], tq)

    t = i * tq + (lax.broadcasted_iota(jnp.int32, (R, LANES), 0) & (tq - 1))
    c = lax.broadcasted_iota(jnp.int32, (R, LANES), 1)
    dist = t - (c * CMP_STRIDE + (CMP_LEN - 1))
    valid = (dist >= 0) & (c < n_cmp)
    s = _dot_nt(qs, kc_ref[...])
    s = jnp.where(valid, s - slope_col * dist.astype(F32), NEG_INF)
    m = jnp.max(s, axis=-1, keepdims=True)
    e = jnp.exp(s - m)
    p_c = e / jnp.sum(e, axis=-1, keepdims=True)
    any_valid = (t[:, 0:1] >= CMP_LEN - 1).astype(F32)
    p_c = p_c * any_valid
    o = _dot(p_c.astype(BF16), vc_ref[...])
    lane = lax.broadcasted_iota(jnp.int32, (tq, LANES), 1)
    lo_half = lane < HEAD_DIM
    for p in range(nh // 2):
        pair = jnp.where(lo_half, o[p * tq:(p + 1) * tq, :],
                         o[(nh // 2 + p) * tq:(nh // 2 + p + 1) * tq, :])
        o_ref[:, p * LANES:(p + 1) * LANES] = pair.astype(o_ref.dtype)

    p_sum = p_c[0:tq, :]
    for h in range(1, nh):
        p_sum = p_sum + p_c[h * tq:(h + 1) * tq, :]
    jj = lax.broadcasted_iota(jnp.int32, (n_blk, LANES), 0)
    cc = lax.broadcasted_iota(jnp.int32, (n_blk, LANES), 1)
    overlap = (jnp.minimum(cc * CMP_STRIDE + CMP_LEN, jj * SLC_BLOCK + SLC_BLOCK)
               - jnp.maximum(cc * CMP_STRIDE, jj * SLC_BLOCK))
    overlap = jnp.where(cc < n_cmp, jnp.maximum(overlap, 0), 0)
    w_t = (overlap.astype(F32) * (1.0 / CMP_LEN)).astype(BF16)
    hi = p_sum.astype(BF16)
    rem = p_sum - hi.astype(F32)
    mid = rem.astype(BF16)
    low = (rem - mid.astype(F32)).astype(BF16)
    score = _dot_nt(w_t, hi) + _dot_nt(w_t, mid) + _dot_nt(w_t, low)

    tpos = i * tq + lax.broadcasted_iota(jnp.int32, (n_blk, tq), 1)
    t_blk = tpos >> _log2(SLC_BLOCK)
    jb = lax.broadcasted_iota(jnp.int32, (n_blk, tq), 0)
    score = jnp.where((jb == 0) | (jb == t_blk) | (jb == t_blk - 1), FORCED_SCORE, score)
    score = jnp.where(jb > t_blk, -1.0, score)
    rank = jnp.zeros((n_blk, tq), F32)
    for kk in range(n_blk):
        row = score[kk:kk + 1, :]
        ahead = (row > score) | ((row == score) & (kk < jb))
        rank = rank + ahead.astype(F32)
    sel_t = (rank < n_sel).astype(BF16)
    sel_t = jnp.concatenate([sel_t, jnp.zeros((LANES - n_blk, tq), BF16)], axis=0)
    eye = (lax.broadcasted_iota(jnp.int32, (tq, tq), 0)
           == lax.broadcasted_iota(jnp.int32, (tq, tq), 1)).astype(BF16)
    sel_ref[...] = _dot_nt(eye, sel_t).astype(sel_ref.dtype)


def _cmp_attn(proj, kvc, slopes, B, S, *, tq=512):
    nq = S // tq
    G = NSA_KV_GROUPS
    n_cmp = (S - CMP_LEN) // CMP_STRIDE + 1
    n_blk = S // SLC_BLOCK
    n_sel = min(SLC_TOPK, n_blk)
    kern = functools.partial(_cmp_attn_kernel, tq=tq, n_cmp=n_cmp, n_blk=n_blk, n_sel=n_sel)
    return pl.pallas_call(
        kern,
        grid=(B, G, nq),
        in_specs=[_smem_spec(),
                  pl.BlockSpec((tq, 2 * LANES), lambda b, g, i: (b * nq + i, BLK_BQ // 2 + g)),
                  pl.BlockSpec((None, LANES, LANES), lambda b, g, i: (0, b * G + g, 0)),
                  pl.BlockSpec((None, LANES, LANES), lambda b, g, i: (1, b * G + g, 0))],
        out_specs=[pl.BlockSpec((tq, 2 * LANES), lambda b, g, i: (b * nq + i, g)),
                   pl.BlockSpec((tq, LANES), lambda b, g, i: (b * nq + i, g))],
        out_shape=[jax.ShapeDtypeStruct((B * S, BRANCH_WIDTH), F32),
                   jax.ShapeDtypeStruct((B * S, G * LANES), BF16)],
        compiler_params=_cparams(("parallel", "parallel", "parallel")),
        name="nsa_cmp_attn",
    )(slopes, proj, kvc, kvc)


def _interleave_matrix(tm, n_outer):
    e_cnt = tm // n_outer
    r_out = lax.broadcasted_iota(jnp.int32, (tm, tm), 0)
    r_in = lax.broadcasted_iota(jnp.int32, (tm, tm), 1)
    outer = r_in >> _log2(e_cnt)
    inner = r_in & (e_cnt - 1)
    return (r_out == inner * n_outer + outer).astype(BF16)


def _permute_rows(pm, x):
    if x.dtype == BF16:
        return _dot(pm, x)
    hi = x.astype(BF16)
    rem = x - hi.astype(F32)
    mid = rem.astype(BF16)
    low = (rem - mid.astype(F32)).astype(BF16)
    return _dot(pm, hi) + _dot(pm, mid) + _dot(pm, low)


def _merge_kernel(x_ref, oa_ref, ocmp_ref, oslc_ref, owin_ref, oc_ref, gts_ref,
                  wmg_ref, wbr_ref, wout_ref, g_ref, b_ref, o_ref):
    tm = x_ref.shape[0]
    x = x_ref[...]
    xb = x.astype(BF16)
    p_slc = _interleave_matrix(tm, oslc_ref.shape[0])
    p_band = _interleave_matrix(tm, owin_ref.shape[0])
    o_slc = _permute_rows(p_slc, oslc_ref[...].reshape(tm, BRANCH_WIDTH))
    o_win = _permute_rows(p_band, owin_ref[...].reshape(tm, BRANCH_WIDTH))
    o_c = _permute_rows(p_band, oc_ref[...].reshape(tm, BRANCH_WIDTH)).astype(BF16)
    gates = jax.nn.sigmoid(gts_ref[...])
    lane = lax.broadcasted_iota(jnp.int32, (tm, LANES), 1)
    lo_half = lane < HEAD_DIM
    nsa = (ocmp_ref[...], o_slc, o_win)
    ob_tiles = []
    for pair in range(NSA_HEADS // 2):
        acc = jnp.zeros((tm, LANES), F32)
        for n in range(3):
            c0 = n * NSA_HEADS + 2 * pair
            gate = jnp.where(lo_half, gates[:, c0:c0 + 1], gates[:, c0 + 1:c0 + 2])
            acc = acc + gate * nsa[n][:, pair * LANES:(pair + 1) * LANES]
        ob_tiles.append(acc.astype(BF16))
    ob = jnp.concatenate(ob_tiles, axis=1)
    branches = (oa_ref[...], ob, o_c)
    y = jnp.zeros((tm, D_MODEL), F32)
    for n in range(N_BRANCHES):
        mg = jax.nn.sigmoid(_dot(xb, wmg_ref[:, n * D_MODEL:(n + 1) * D_MODEL]))
        y = y + mg * _dot(branches[n], wbr_ref[n])
    mix = _dot(y.astype(BF16), wout_ref[...])
    o_ref[...] = _layer_norm(ALPHA * x + mix, g_ref[...], b_ref[...])


def _merge(x, o_a, o_cmp, o_slc, o_win, o_c, projf, w_mg, w_br, w_out, ln_g, ln_b, B, S,
           *, tm=512):
    T, D = x.shape
    nt = S // tm
    n_slc, n_band = S // SLC_QUERY_CHUNK, S // QUERY_BLOCK
    o_slc = o_slc.reshape(B, n_slc, SLC_QUERY_CHUNK, BRANCH_WIDTH)
    o_win = o_win.reshape(B, n_band, QUERY_BLOCK, BRANCH_WIDTH)
    o_c = o_c.reshape(B, n_band, QUERY_BLOCK, BRANCH_WIDTH)
    row = lambda b, a: (b * nt + a, 0)
    const2 = lambda b, a: (0, 0)
    chunked = lambda b, a: (b, 0, a, 0)
    return pl.pallas_call(
        _merge_kernel,
        grid=(B, nt),
        in_specs=[pl.BlockSpec((tm, D), row),
                  pl.BlockSpec((tm, BRANCH_WIDTH), row),
                  pl.BlockSpec((tm, BRANCH_WIDTH), row),
                  pl.BlockSpec((None, n_slc, tm // n_slc, BRANCH_WIDTH), chunked),
                  pl.BlockSpec((None, n_band, tm // n_band, BRANCH_WIDTH), chunked),
                  pl.BlockSpec((None, n_band, tm // n_band, BRANCH_WIDTH), chunked),
                  pl.BlockSpec((tm, LANES), lambda b, a: (b * nt + a, 2)),
                  pl.BlockSpec((D, N_BRANCHES * D), const2),
                  pl.BlockSpec((N_BRANCHES, BRANCH_WIDTH, D), lambda b, a: (0, 0, 0)),
                  pl.BlockSpec((D, D), const2),
                  pl.BlockSpec((1, D), const2),
                  pl.BlockSpec((1, D), const2)],
        out_specs=pl.BlockSpec((tm, D), row),
        out_shape=jax.ShapeDtypeStruct((T, D), F32),
        compiler_params=_cparams(("parallel", "parallel")),
        name="merge",
    )(x, o_a, o_cmp, o_slc, o_win, o_c, projf, w_mg, w_br, w_out, ln_g, ln_b)


def _dup_kv(w, groups):
    D = w.shape[0]
    w = w.reshape(D, 2 * groups, 1, HEAD_DIM)
    return jnp.broadcast_to(w, (D, 2 * groups, 2, HEAD_DIM)).reshape(D, 4 * groups * HEAD_DIM)


def _split_w_in(w):
    sizes = (512, 512, 512, 512, 256, 256, 256, 24, 512, 128, 128, 3072)
    offs = [0]
    for n in sizes:
        offs.append(offs[-1] + n)
    (a_q, a_k, a_v, b_q, b_kvc, b_kvs, b_kvw, b_g, c_q, c_k, c_v, m_g) = [
        w[:, offs[n]:offs[n + 1]] for n in range(len(sizes))]
    w_bf = jnp.concatenate(
        [a_q, a_k, a_v, b_q, c_q, _dup_kv(b_kvs, 2), _dup_kv(b_kvw, 2),
         _dup_kv(jnp.concatenate([c_k, c_v], axis=1), 2)], axis=1).astype(BF16)
    pad = jnp.zeros((w.shape[0], N_PROJ_F32 - 256 - 24), w.dtype)
    w_f32 = jnp.concatenate([b_kvc, b_g, pad], axis=1).astype(BF16)
    return w_bf, w_f32, m_g.astype(BF16)


def _alibi_slopes(n_heads):
    assert 8 % n_heads == 0
    return jnp.exp2(-8.0 * jnp.arange(1, n_heads + 1, dtype=F32) / n_heads)


def _token_mixing(x, B, S, w_in, diff_lam, diff_subln_g, cmp_pos, cmp_w1, cmp_w2, sinks,
                  w_branch, w_out, ln_g, ln_b, lambda_init):
    G = NSA_KV_GROUPS
    w_bf, w_f32, w_mg = _split_w_in(w_in)
    proj, projf = _proj(x, w_bf, w_f32)

    n_rows = S // CMP_STRIDE
    r = projf[:, :4 * HEAD_DIM].reshape(B, S, 2, G, HEAD_DIM).transpose(2, 0, 3, 1, 4)
    r = r.reshape(2, B * G * n_rows, CMP_STRIDE * HEAD_DIM)
    pos = cmp_pos.reshape(2, 2, CMP_STRIDE * HEAD_DIM)
    w2dup = jnp.concatenate([cmp_w2, cmp_w2], axis=-1).astype(BF16)
    kvc = _compress(r, pos, cmp_w1.astype(BF16), w2dup)

    slopes8 = _alibi_slopes(NSA_HEADS)
    o_a = _diff_attn(proj, _alibi_slopes(DIFF_HEADS), diff_lam, diff_subln_g.reshape(1, LANES),
                     B, S, lambda_init, tq=512, tk=512)
    o_cmp, sel = _cmp_attn(proj, kvc, slopes8, B, S)
    o_slc = _slc_attn(proj, slopes8, sel, B, S, tq=512, tk=512)
    o_win = _window_attn(proj, slopes8, B, S, q_blk=BLK_BQ, kv_blk=BLK_WIN, window=NSA_WINDOW,
                         out_dtype=BF16, tq=1024, tsub=128, name="nsa_win_attn")
    o_c = _window_attn(proj, _alibi_slopes(SWA_HEADS), B, S, q_blk=BLK_CQ, kv_blk=BLK_SWA,
                       window=SWA_WINDOW, sinks=sinks, out_dtype=BF16, tq=1024, tsub=128,
                       name="swa_attn")
    return _merge(x, o_a, o_cmp, o_slc, o_win, o_c, projf, w_mg, w_branch.astype(BF16),
                  w_out.astype(BF16), ln_g, ln_b, B, S)


def kernel(x, p, ffn_w_in, ffn_w_out, ln_g, ln_b, w_in, diff_lam, diff_subln_g, nsa_cmp_pos,
           nsa_cmp_w1, nsa_cmp_w2, swa_sinks, w_branch, w_out, ple_w_in, ple_w_gate):
    B, S, D = x.shape
    T = B * S
    h = x.reshape(T, D)
    for i in range(DEPTH):
        lambda_init = 0.8 - 0.6 * math.exp(-0.3 * i)
        lg = ln_g[i].reshape(3, 1, D)
        lb = ln_b[i].reshape(3, 1, D)
        h = _ffn(h, ffn_w_in[i, 0].astype(BF16), ffn_w_out[i, 0].astype(BF16), lg[0], lb[0])
        h = _token_mixing(h, B, S, w_in[i], diff_lam[i], diff_subln_g[i], nsa_cmp_pos[i],
                          nsa_cmp_w1[i], nsa_cmp_w2[i], swa_sinks[i], w_branch[i], w_out[i],
                          lg[1], lb[1], lambda_init)
        h = _ffn(h, ffn_w_in[i, 1].astype(BF16), ffn_w_out[i, 1].astype(BF16), lg[2], lb[2],
                 ple_args=(p[i].reshape(T, PLE_DIM), ple_w_gate[i].astype(BF16),
                           ple_w_in[i].astype(BF16)))
    return h.reshape(B, S, D)
```

```python
import functools
import math

import jax
import jax.numpy as jnp
from jax import lax
from jax.experimental import pallas as pl
from jax.experimental.pallas import tpu as pltpu

F32 = jnp.float32
BF16 = jnp.bfloat16

D_MODEL = 1024
DEPTH = 2
HEAD_DIM = 64
DIFF_HEADS = 4
NSA_HEADS = 8
NSA_KV_GROUPS = 2
CMP_LEN = 32
CMP_STRIDE = 16
CMP_HIDDEN = 256
SLC_BLOCK = 64
SLC_TOPK = 16
NSA_WINDOW = 512
FORCED_SCORE = 1e4
SWA_HEADS = 8
SWA_KV_HEADS = 2
SWA_WINDOW = 128
QUERY_BLOCK = 128
SLC_QUERY_CHUNK = 32
N_BRANCHES = 3
BRANCH_WIDTH = 512
D_FF = 2816
PLE_DIM = 256
LN_EPS = 1e-5
SUBLN_EPS = 1e-5
ALPHA = (2 * DEPTH) ** 0.25
NEG_INF = -1e30
QK_SCALE = HEAD_DIM ** -0.5

LANES = 128
VMEM_LIMIT = 56 * 1024 * 1024

BLK_AQ, BLK_AK, BLK_AV = 0, 4, 8
BLK_BQ, BLK_CQ = 12, 16
BLK_SLC, BLK_WIN, BLK_SWA = 20, 24, 28
N_PROJ_BF = 32 * LANES
N_PROJ_F32 = 3 * LANES


def _cparams(sem):
    return pltpu.CompilerParams(dimension_semantics=sem, vmem_limit_bytes=VMEM_LIMIT)


def _layer_norm(h, g, b):
    mu = jnp.mean(h, axis=-1, keepdims=True)
    d = h - mu
    var = jnp.mean(d * d, axis=-1, keepdims=True)
    return d * lax.rsqrt(var + LN_EPS) * g + b


def _dot(a, b):
    return jnp.dot(a, b, preferred_element_type=F32)


def _dot_nt(a, b):
    return lax.dot_general(a, b, (((1,), (1,)), ((), ())), preferred_element_type=F32)


FF_CHUNK = 256
EPILOGUE_ROWS = 256


def _ffn_kernel(*refs, ple):
    if ple:
        x_ref, win_ref, wout_ref, g_ref, b_ref, p_ref, pwg_ref, pwi_ref, o_ref, acc_ref = refs
    else:
        x_ref, win_ref, wout_ref, g_ref, b_ref, o_ref, acc_ref = refs
    xb = x_ref[...].astype(BF16)
    for k in range(D_FF // FF_CHUNK):
        c0 = k * FF_CHUNK
        gate = _dot(xb, win_ref[:, c0:c0 + FF_CHUNK])
        up = _dot(xb, win_ref[:, D_FF + c0:D_FF + c0 + FF_CHUNK])
        hid = ((gate * jax.nn.sigmoid(gate)) * up).astype(BF16)
        contrib = _dot(hid, wout_ref[c0:c0 + FF_CHUNK, :])
        if k == 0:
            acc_ref[...] = contrib
        else:
            acc_ref[...] += contrib
    for r0 in range(0, x_ref.shape[0], EPILOGUE_ROWS):
        rows = slice(r0, r0 + EPILOGUE_ROWS)
        h = ALPHA * x_ref[rows, :] + 0.5 * acc_ref[rows, :]
        if ple:
            gp = jax.nn.sigmoid(_dot(h.astype(BF16), pwg_ref[...]))
            h = h + gp * _dot(p_ref[rows, :].astype(BF16), pwi_ref[...])
        o_ref[rows, :] = _layer_norm(h, g_ref[...], b_ref[...])


def _resident(shape, lead=()):
    block = (None,) * len(lead) + tuple(shape)
    index = tuple(lead) + (0,) * len(shape)
    return pl.BlockSpec(block, lambda i: index, pipeline_mode=pl.Buffered(1))


def _ffn(x, w_in, w_out, which, ln_g, ln_b, ple_args=None, *, tm=512):
    T, D = x.shape
    ple = ple_args is not None
    in_specs = [pl.BlockSpec((tm, D), lambda i: (i, 0)),
                _resident(w_in.shape[2:], which), _resident(w_out.shape[2:], which),
                _resident((1, D)), _resident((1, D))]
    args = [x, w_in, w_out, ln_g, ln_b]
    if ple:
        p, pwg, pwi = ple_args
        layer = which[0]
        in_specs += [pl.BlockSpec((None, tm, PLE_DIM), lambda i: (layer, i, 0)),
                     _resident(pwg.shape[1:], (layer,)), _resident(pwi.shape[1:], (layer,))]
        args += [p, pwg, pwi]
    return pl.pallas_call(
        functools.partial(_ffn_kernel, ple=ple),
        grid=(T // tm,),
        in_specs=in_specs,
        out_specs=pl.BlockSpec((tm, D), lambda i: (i, 0)),
        out_shape=jax.ShapeDtypeStruct((T, D), F32),
        scratch_shapes=[pltpu.VMEM((tm, D), F32)],
        compiler_params=_cparams(("parallel",)),
        name="ffn_ple" if ple else "ffn",
    )(*args)


PROJ_COLS = 1024


def _proj_kernel(x_ref, wb_ref, wf_ref, ob_ref, of_ref):
    xb = x_ref[...].astype(BF16)
    for n0 in range(0, wb_ref.shape[1], PROJ_COLS):
        ob_ref[:, n0:n0 + PROJ_COLS] = _dot(xb, wb_ref[:, n0:n0 + PROJ_COLS]).astype(BF16)
    of_ref[...] = _dot(xb, wf_ref[...])


def _proj(x, w_bf, w_f32, *, tm=512):
    T, K = x.shape
    nb, nf = w_bf.shape[1], w_f32.shape[1]
    return pl.pallas_call(
        _proj_kernel,
        grid=(T // tm,),
        in_specs=[pl.BlockSpec((tm, K), lambda i: (i, 0)),
                  pl.BlockSpec((K, nb), lambda i: (0, 0)),
                  pl.BlockSpec((K, nf), lambda i: (0, 0))],
        out_specs=[pl.BlockSpec((tm, nb), lambda i: (i, 0)),
                   pl.BlockSpec((tm, nf), lambda i: (i, 0))],
        out_shape=[jax.ShapeDtypeStruct((T, nb), BF16), jax.ShapeDtypeStruct((T, nf), F32)],
        compiler_params=_cparams(("parallel",)),
        name="proj",
    )(x, w_bf, w_f32)


def _compress_kernel(r_ref, pos_ref, w1_ref, w2_ref, o_ref):
    r = r_ref[...]
    half = CMP_STRIDE * HEAD_DIM
    lo = _dot((r + pos_ref[0:1, :]).astype(BF16), w1_ref[0:half, :])
    hi = _dot((r + pos_ref[1:2, :]).astype(BF16), w1_ref[half:2 * half, :])
    pre = lo + pltpu.roll(hi, hi.shape[0] - 1, 0)
    hid = jax.nn.gelu(pre, approximate=True)
    o_ref[...] = _dot(hid.astype(BF16), w2_ref[...]).astype(o_ref.dtype)


def _compress(r, pos, w1, w2dup, *, tm=512):
    _, M, F = r.shape
    tm = min(tm, M)
    return pl.pallas_call(
        _compress_kernel,
        grid=(2, M // tm),
        in_specs=[pl.BlockSpec((None, tm, F), lambda n, i: (n, i, 0)),
                  pl.BlockSpec((None, 2, F), lambda n, i: (n, 0, 0)),
                  pl.BlockSpec((None, 2 * F, CMP_HIDDEN), lambda n, i: (n, 0, 0)),
                  pl.BlockSpec((None, CMP_HIDDEN, LANES), lambda n, i: (n, 0, 0))],
        out_specs=pl.BlockSpec((None, tm, LANES), lambda n, i: (n, i, 0)),
        out_shape=jax.ShapeDtypeStruct((2, M, LANES), BF16),
        compiler_params=_cparams(("parallel", "parallel")),
        name="nsa_compress",
    )(r, pos, w1, w2dup)


def _log2(n):
    assert n > 0 and n & (n - 1) == 0, n
    return n.bit_length() - 1


def _stack_heads(q_tiles):
    tq = q_tiles[0].shape[0]
    lane = lax.broadcasted_iota(jnp.int32, (tq, LANES), 1)
    lo = lane < HEAD_DIM
    zero = jnp.zeros((tq, LANES), BF16)
    scaled = [q * jnp.asarray(QK_SCALE, BF16) for q in q_tiles]
    parts = [jnp.where(lo, qs, zero) for qs in scaled] + [jnp.where(lo, zero, qs) for qs in scaled]
    return jnp.concatenate(parts, axis=0)


def _stacked_head_order(nh):
    return list(range(0, nh, 2)) + list(range(1, nh, 2))


def _head_column(values, tq):
    nh = len(values)
    head = lax.broadcasted_iota(jnp.int32, (nh * tq, 1), 0) >> _log2(tq)
    col = jnp.zeros((nh * tq, 1), F32)
    for h in range(nh):
        col = jnp.where(head == h, values[h], col)
    return col


KX_ONE_A, KX_COL_HI, KX_COL_LO, KX_ONE_B = 32, 33, 34, 35
POS_SPLIT = 16


def _key_extras(S, tk):
    assert S // SLC_BLOCK <= KX_ONE_A
    key = jnp.arange(S, dtype=jnp.int32)[:, None]
    lane = jnp.arange(LANES, dtype=jnp.int32)[None, :]
    col = key % tk
    x = jnp.where(lane == key // SLC_BLOCK, -NEG_INF, 0.0)
    x = jnp.where((lane == KX_ONE_A) | (lane == KX_ONE_B), 1.0, x)
    x = jnp.where(lane == KX_COL_HI, (col - col % POS_SPLIT).astype(F32), x)
    x = jnp.where(lane == KX_COL_LO, (col % POS_SPLIT).astype(F32), x)
    return x.astype(BF16)


SOFTMAX_ROWS = 32


def _attn_kernel(*refs, mode, tq, tk, use_sel, nh, lambda_init):
    refs = list(refs)
    slopes_ref = refs.pop(0)
    q_ref, k_ref, v_ref, kx_ref = refs.pop(0), refs.pop(0), refs.pop(0), refs.pop(0)
    sel_ref = refs.pop(0) if use_sel else None
    if mode == "diff":
        lam_ref, subg_ref = refs.pop(0), refs.pop(0)
    o_ref = refs.pop(0)
    mask_ref, qs_ref, s_ref, p_ref, m_ref, acc_ref, shift_ref = refs
    hg = pl.program_id(1)
    i = pl.program_id(2)
    R = nh * tq
    gqa = mode == "gqa"
    ratio = tk // tq
    half = R // 2

    xl = lax.broadcasted_iota(jnp.int32, (R, LANES), 1)

    @pl.when(i == 0)
    def _():
        if gqa:
            slope_col = _head_column([slopes_ref[hg * nh + h] for h in _stacked_head_order(nh)], tq)
        else:
            slope_col = jnp.full((R, 1), slopes_ref[hg], F32)
        shift_ref[...] = slope_col * float(tk)
        row = lax.broadcasted_iota(jnp.int32, (R, tk), 0) & (tq - 1)
        col = lax.broadcasted_iota(jnp.int32, (R, tk), 1)
        for a in range(ratio):
            mask_ref[a] = jnp.where(row - col + a * tq >= 0, 0.0, NEG_INF)
        xr = lax.broadcasted_iota(jnp.int32, (R, LANES), 0) & (tq - 1)
        xr_lo = xr & (POS_SPLIT - 1)
        qx = jnp.where(xl == KX_ONE_A, -slope_col * (xr - xr_lo).astype(F32), 0.0)
        qx = jnp.where(xl == KX_ONE_B, -slope_col * xr_lo.astype(F32), qx)
        qx = jnp.where((xl == KX_COL_HI) | (xl == KX_COL_LO), slope_col, qx)
        qs_ref[:, LANES:2 * LANES] = qx.astype(BF16)

    if gqa:
        qs_ref[:, 0:LANES] = _stack_heads([q_ref[:, p * LANES:(p + 1) * LANES]
                                           for p in range(nh // 2)])
    else:
        qs_ref[:, 0:LANES] = _stack_heads([q_ref[...]])
    if use_sel:
        n_blk = k_ref.shape[0] // SLC_BLOCK
        not_sel = jnp.concatenate([(sel_ref[...].astype(F32) - 1.0).astype(BF16)] * nh, axis=0)
        qs_ref[:, LANES:2 * LANES] = jnp.where(xl < n_blk, not_sel, qs_ref[:, LANES:2 * LANES])
    m_ref[...] = jnp.full((R, 1), NEG_INF, F32)
    acc_ref[...] = jnp.zeros(acc_ref.shape, F32)
    lo_half_k = lax.broadcasted_iota(jnp.int32, (tk, LANES), 1) < HEAD_DIM

    def scores(j, slot):
        start = pl.multiple_of(j * tk, tk)
        kk = jnp.concatenate([k_ref[pl.ds(start, tk), :], kx_ref[pl.ds(start, tk), :]], axis=1)
        for h0 in (0, half):
            s_ref[slot, h0:h0 + half, :] = _dot_nt(qs_ref[h0:h0 + half, :], kk)

    def softmax_pv(j, slot, mask_idx):
        start = pl.multiple_of(j * tk, tk)
        v = v_ref[pl.ds(start, tk), :]
        one = jnp.ones((tk, LANES), BF16)
        if gqa:
            v_halves = (jnp.where(lo_half_k, v, one), jnp.where(lo_half_k, one, v))
        else:
            v_halves = (jnp.concatenate([v, one], axis=1),) * 2
        for hh, h0 in enumerate((0, half)):
            for c in range(half // SOFTMAX_ROWS):
                r0 = h0 + c * SOFTMAX_ROWS
                rows = slice(r0, r0 + SOFTMAX_ROWS)
                s = s_ref[slot, rows, :]
                if mask_idx is not None:
                    s = s + mask_ref[mask_idx, rows, :]
                m_prev = m_ref[rows, :] - shift_ref[rows, :]
                m_new = jnp.maximum(m_prev, jnp.max(s, axis=-1, keepdims=True))
                alpha = jnp.exp(m_prev - m_new)
                p_ref[rows, :] = jnp.exp(s - m_new).astype(BF16)
                m_ref[rows, :] = m_new
                acc_ref[rows, :] = alpha * acc_ref[rows, :]
            acc_ref[h0:h0 + half, :] += _dot(p_ref[h0:h0 + half, :], v_halves[hh])

    j_last = i // ratio
    scores(0, 0)

    def pair(t, c):
        scores(2 * t + 1, 1)
        softmax_pv(2 * t, 0, None)
        scores(2 * t + 2, 0)
        softmax_pv(2 * t + 1, 1, None)
        return c

    lax.fori_loop(0, j_last // 2, pair, 0)

    @pl.when(j_last % 2 == 1)
    def _():
        scores(j_last, 1)
        softmax_pv(j_last - 1, 0, None)
        softmax_pv(j_last, 1, i % ratio)

    @pl.when(j_last % 2 == 0)
    def _():
        softmax_pv(j_last, 0, i % ratio)

    acc = acc_ref[...]
    if gqa:
        acc_e, acc_o = acc[:half], acc[half:]
        l_e = acc_e[:, HEAD_DIM:HEAD_DIM + 1]
        l_o = acc_o[:, 0:1]
        o_e = acc_e / l_e
        o_o = acc_o / l_o
        lo_half_q = lax.broadcasted_iota(jnp.int32, (tq, LANES), 1) < HEAD_DIM
        for p_ in range(nh // 2):
            pair = jnp.where(lo_half_q, o_e[p_ * tq:(p_ + 1) * tq, :], o_o[p_ * tq:(p_ + 1) * tq, :])
            o_ref[:, p_ * LANES:(p_ + 1) * LANES] = pair.astype(o_ref.dtype)
    else:
        o = acc[:, 0:LANES] / acc[:, LANES:LANES + 1]
        lam = lam_ref[...]
        lam_full = (jnp.exp(jnp.sum(lam[0:1, :] * lam[1:2, :], axis=-1, keepdims=True))
                    - jnp.exp(jnp.sum(lam[2:3, :] * lam[3:4, :], axis=-1, keepdims=True))
                    + lambda_init)
        od = o[0:tq, :] - lam_full * o[tq:2 * tq, :]
        od = od * lax.rsqrt(jnp.mean(od * od, axis=-1, keepdims=True) + SUBLN_EPS) * subg_ref[...]
        o_ref[...] = (od * (1.0 - lambda_init)).astype(o_ref.dtype)


def _attn_scratch(n_masks, R, tk, acc_width):
    return [pltpu.VMEM((n_masks, R, tk), F32),
            pltpu.VMEM((R, 2 * LANES), BF16),
            pltpu.VMEM((2, R, tk), F32),
            pltpu.VMEM((R, tk), BF16),
            pltpu.VMEM((R, 1), F32),
            pltpu.VMEM((R, acc_width), F32),
            pltpu.VMEM((R, 1), F32)]


def _window_kernel(*refs, tq, tsub, back, window, use_sink, nh, n_special):
    refs = list(refs)
    slopes_ref = refs.pop(0)
    sinks_ref = refs.pop(0) if use_sink else None
    q_ref, k_ref, v_ref, o_ref, bias_ref, s_ref, p_ref, m_ref, sink_ref = refs
    g = pl.program_id(1)
    i = pl.program_id(2)
    R = nh * tsub
    half = R // 2
    span = back + tsub
    n_sub = tq // tsub
    order = _stacked_head_order(nh)

    @pl.when(i == 0)
    def _():
        slope_col = _head_column([slopes_ref[g * nh + h] for h in order], tsub)
        if use_sink:
            sink_ref[...] = _head_column([sinks_ref[g * nh + h] for h in order], tsub)
        row = lax.broadcasted_iota(jnp.int32, (R, span), 0) & (tsub - 1)
        col = lax.broadcasted_iota(jnp.int32, (R, span), 1)
        for a in range(n_special + 1):
            dist = row - col + (a * tsub if a < n_special else back)
            ok = (dist >= 0) & (dist < window)
            bias_ref[a] = jnp.where(ok, -slope_col * dist.astype(F32), NEG_INF)

    lo_half_k = lax.broadcasted_iota(jnp.int32, (span, LANES), 1) < HEAD_DIM
    lo_half_q = lax.broadcasted_iota(jnp.int32, (tsub, LANES), 1) < HEAD_DIM
    one = jnp.ones((span, LANES), BF16)
    for sub in range(n_sub):
        it = i * n_sub + sub
        a = jnp.minimum(it, n_special)
        start = pl.multiple_of(jnp.maximum(it * tsub - back, 0), LANES)
        q_rows = slice(sub * tsub, (sub + 1) * tsub)
        qs = _stack_heads([q_ref[q_rows, p * LANES:(p + 1) * LANES] for p in range(nh // 2)])
        k = k_ref[pl.ds(start, span), :]
        v = v_ref[pl.ds(start, span), :]
        v_halves = (jnp.where(lo_half_k, v, one), jnp.where(lo_half_k, one, v))
        for h0 in (0, half):
            s_ref[sub, h0:h0 + half, :] = _dot_nt(qs[h0:h0 + half, :], k)
        accs = []
        for hh, h0 in enumerate((0, half)):
            for c in range(half // SOFTMAX_ROWS):
                r0 = h0 + c * SOFTMAX_ROWS
                rows = slice(r0, r0 + SOFTMAX_ROWS)
                s = s_ref[sub, rows, :] + bias_ref[a, rows, :]
                m = jnp.max(s, axis=-1, keepdims=True)
                p_ref[sub, rows, :] = jnp.exp(s - m).astype(BF16)
                m_ref[sub, rows, :] = m
            accs.append(_dot(p_ref[sub, h0:h0 + half, :], v_halves[hh]))

        acc_e, acc_o = accs
        l_e = acc_e[:, HEAD_DIM:HEAD_DIM + 1]
        l_o = acc_o[:, 0:1]
        if use_sink:
            extra = jnp.exp(sink_ref[...] - m_ref[sub])
            l_e = l_e + extra[:half]
            l_o = l_o + extra[half:]
        o_e = acc_e / l_e
        o_o = acc_o / l_o
        for p_ in range(nh // 2):
            pair = jnp.where(lo_half_q, o_e[p_ * tsub:(p_ + 1) * tsub, :],
                             o_o[p_ * tsub:(p_ + 1) * tsub, :])
            o_ref[sub, :, p_ * LANES:(p_ + 1) * LANES] = pair.astype(o_ref.dtype)


def _window_attn(proj, slopes, B, S, *, q_blk, kv_blk, window, sinks=None, out_dtype, tq, tsub,
                 name):
    nq = S // tq
    G, nh = 2, 4
    back = -(-window // LANES) * LANES
    n_special = -(-back // tsub)
    span = back + tsub
    n_sub = tq // tsub
    R = nh * tsub
    assert span <= S and tsub == QUERY_BLOCK
    kern = functools.partial(_window_kernel, tq=tq, tsub=tsub, back=back, window=window,
                             use_sink=sinks is not None, nh=nh, n_special=n_special)
    in_specs = [_smem_spec()]
    args = [slopes]
    if sinks is not None:
        in_specs.append(_smem_spec())
        args.append(sinks)
    in_specs += [
        pl.BlockSpec((tq, 2 * LANES), lambda b, g, i: (b * nq + i, q_blk // 2 + g)),
        pl.BlockSpec((S, LANES), lambda b, g, i: (b, kv_blk + g)),
        pl.BlockSpec((S, LANES), lambda b, g, i: (b, kv_blk + 2 + g)),
    ]
    args += [proj, proj, proj]
    return pl.pallas_call(
        kern,
        grid=(B, G, nq),
        in_specs=in_specs,
        out_specs=pl.BlockSpec((None, n_sub, tsub, 2 * LANES), lambda b, g, i: (b, i, 0, g)),
        out_shape=jax.ShapeDtypeStruct((B, S // tsub, tsub, BRANCH_WIDTH), out_dtype),
        scratch_shapes=[pltpu.VMEM((n_special + 1, R, span), F32),
                        pltpu.VMEM((n_sub, R, span), F32),
                        pltpu.VMEM((n_sub, R, span), BF16),
                        pltpu.VMEM((n_sub, R, 1), F32),
                        pltpu.VMEM((R, 1), F32)],
        compiler_params=_cparams(("parallel", "parallel", "arbitrary")),
        name=name,
    )(*args)


def _smem_spec():
    return pl.BlockSpec(memory_space=pltpu.SMEM)


def _diff_attn(proj, slopes, lam, subg, B, S, lambda_init, *, tq, tk):
    nq = S // tq
    kern = functools.partial(_attn_kernel, mode="diff", tq=tq, tk=tk, use_sel=False, nh=2,
                             lambda_init=lambda_init)
    return pl.pallas_call(
        kern,
        grid=(B, DIFF_HEADS, nq),
        in_specs=[_smem_spec(),
                  pl.BlockSpec((tq, LANES), lambda b, h, i: (b * nq + i, BLK_AQ + h)),
                  pl.BlockSpec((S, LANES), lambda b, h, i: (b, BLK_AK + h)),
                  pl.BlockSpec((S, LANES), lambda b, h, i: (b, BLK_AV + h)),
                  pl.BlockSpec((S, LANES), lambda b, h, i: (0, 0)),
                  pl.BlockSpec((4, HEAD_DIM), lambda b, h, i: (0, 0)),
                  pl.BlockSpec((1, LANES), lambda b, h, i: (0, 0))],
        out_specs=pl.BlockSpec((tq, LANES), lambda b, h, i: (b * nq + i, h)),
        out_shape=jax.ShapeDtypeStruct((B * S, BRANCH_WIDTH), BF16),
        scratch_shapes=_attn_scratch(tk // tq, 2 * tq, tk, 2 * LANES),
        compiler_params=_cparams(("parallel", "parallel", "arbitrary")),
        name="diff_attn",
    )(slopes, proj, proj, proj, _key_extras(S, tk), lam, subg)


def _slc_attn(proj, slopes, sel, B, S, *, tq, tk):
    nq = S // tq
    G, nh = 2, 4
    kern = functools.partial(_attn_kernel, mode="gqa", tq=tq, tk=tk, use_sel=True, nh=nh,
                             lambda_init=0.0)
    in_specs = [
        _smem_spec(),
        pl.BlockSpec((tq, 2 * LANES), lambda b, g, i: (b * nq + i, BLK_BQ // 2 + g)),
        pl.BlockSpec((S, LANES), lambda b, g, i: (b, BLK_SLC + g)),
        pl.BlockSpec((S, LANES), lambda b, g, i: (b, BLK_SLC + 2 + g)),
        pl.BlockSpec((S, LANES), lambda b, g, i: (0, 0)),
        pl.BlockSpec((tq, LANES), lambda b, g, i: (b * nq + i, g)),
    ]
    return pl.pallas_call(
        kern,
        grid=(B, G, nq),
        in_specs=in_specs,
        out_specs=pl.BlockSpec((tq, 2 * LANES), lambda b, g, i: (b * nq + i, g)),
        out_shape=jax.ShapeDtypeStruct((B * S, BRANCH_WIDTH), F32),
        scratch_shapes=_attn_scratch(tk // tq, nh * tq, tk, LANES),
        compiler_params=_cparams(("parallel", "parallel", "arbitrary")),
        name="nsa_slc_attn",
    )(slopes, proj, proj, proj, _key_extras(S, tk), sel)


def _cmp_attn_kernel(slopes_ref, q_ref, kc_ref, vc_ref, o_ref, sel_ref, *, tq, n_cmp, n_blk, n_sel):
    g = pl.program_id(1)
    i = pl.program_id(2)
    nh = NSA_HEADS // NSA_KV_GROUPS
    R = nh * tq
    qs = _stack_heads([q_ref[:, p * LANES:(p + 1) * LANES] for p in range(nh // 2)])
    slope_col = _head_column([slopes_ref[g * nh + h] for h in _stacked_head_order(nh)], tq)

    t = i * tq + (lax.broadcasted_iota(jnp.int32, (R, LANES), 0) & (tq - 1))
    c = lax.broadcasted_iota(jnp.int32, (R, LANES), 1)
    dist = t - (c * CMP_STRIDE + (CMP_LEN - 1))
    valid = (dist >= 0) & (c < n_cmp)
    s = _dot_nt(qs, kc_ref[...])
    s = jnp.where(valid, s - slope_col * dist.astype(F32), NEG_INF)
    m = jnp.max(s, axis=-1, keepdims=True)
    e = jnp.exp(s - m)
    p_c = e / jnp.sum(e, axis=-1, keepdims=True)
    any_valid = (t[:, 0:1] >= CMP_LEN - 1).astype(F32)
    p_c = p_c * any_valid
    o = _dot(p_c.astype(BF16), vc_ref[...])
    lane = lax.broadcasted_iota(jnp.int32, (tq, LANES), 1)
    lo_half = lane < HEAD_DIM
    for p in range(nh // 2):
        pair = jnp.where(lo_half, o[p * tq:(p + 1) * tq, :],
                         o[(nh // 2 + p) * tq:(nh // 2 + p + 1) * tq, :])
        o_ref[:, p * LANES:(p + 1) * LANES] = pair.astype(o_ref.dtype)

    p_sum = p_c[0:tq, :]
    for h in range(1, nh):
        p_sum = p_sum + p_c[h * tq:(h + 1) * tq, :]
    jj = lax.broadcasted_iota(jnp.int32, (n_blk, LANES), 0)
    cc = lax.broadcasted_iota(jnp.int32, (n_blk, LANES), 1)
    overlap = (jnp.minimum(cc * CMP_STRIDE + CMP_LEN, jj * SLC_BLOCK + SLC_BLOCK)
               - jnp.maximum(cc * CMP_STRIDE, jj * SLC_BLOCK))
    overlap = jnp.where(cc < n_cmp, jnp.maximum(overlap, 0), 0)
    w_t = (overlap.astype(F32) * (1.0 / CMP_LEN)).astype(BF16)
    hi = p_sum.astype(BF16)
    rem = p_sum - hi.astype(F32)
    mid = rem.astype(BF16)
    low = (rem - mid.astype(F32)).astype(BF16)
    score = _dot_nt(w_t, hi) + _dot_nt(w_t, mid) + _dot_nt(w_t, low)

    tpos = i * tq + lax.broadcasted_iota(jnp.int32, (n_blk, tq), 1)
    t_blk = tpos >> _log2(SLC_BLOCK)
    jb = lax.broadcasted_iota(jnp.int32, (n_blk, tq), 0)
    score = jnp.where((jb == 0) | (jb == t_blk) | (jb == t_blk - 1), FORCED_SCORE, score)
    score = jnp.where(jb > t_blk, -1.0, score)
    rank = jnp.zeros((n_blk, tq), F32)
    for kk in range(n_blk):
        row = score[kk:kk + 1, :]
        ahead = (row > score) | ((row == score) & (kk < jb))
        rank = rank + ahead.astype(F32)
    sel_t = (rank < n_sel).astype(BF16)
    sel_t = jnp.concatenate([sel_t, jnp.zeros((LANES - n_blk, tq), BF16)], axis=0)
    eye = (lax.broadcasted_iota(jnp.int32, (tq, tq), 0)
           == lax.broadcasted_iota(jnp.int32, (tq, tq), 1)).astype(BF16)
    sel_ref[...] = _dot_nt(eye, sel_t).astype(sel_ref.dtype)


def _cmp_attn(proj, kvc, slopes, B, S, *, tq=512):
    nq = S // tq
    G = NSA_KV_GROUPS
    n_cmp = (S - CMP_LEN) // CMP_STRIDE + 1
    n_blk = S // SLC_BLOCK
    n_sel = min(SLC_TOPK, n_blk)
    kern = functools.partial(_cmp_attn_kernel, tq=tq, n_cmp=n_cmp, n_blk=n_blk, n_sel=n_sel)
    return pl.pallas_call(
        kern,
        grid=(B, G, nq),
        in_specs=[_smem_spec(),
                  pl.BlockSpec((tq, 2 * LANES), lambda b, g, i: (b * nq + i, BLK_BQ // 2 + g)),
                  pl.BlockSpec((None, LANES, LANES), lambda b, g, i: (0, b * G + g, 0)),
                  pl.BlockSpec((None, LANES, LANES), lambda b, g, i: (1, b * G + g, 0))],
        out_specs=[pl.BlockSpec((tq, 2 * LANES), lambda b, g, i: (b * nq + i, g)),
                   pl.BlockSpec((tq, LANES), lambda b, g, i: (b * nq + i, g))],
        out_shape=[jax.ShapeDtypeStruct((B * S, BRANCH_WIDTH), F32),
                   jax.ShapeDtypeStruct((B * S, G * LANES), BF16)],
        compiler_params=_cparams(("parallel", "parallel", "parallel")),
        name="nsa_cmp_attn",
    )(slopes, proj, kvc, kvc)


def _interleave_matrix(tm, n_outer):
    e_cnt = tm // n_outer
    r_out = lax.broadcasted_iota(jnp.int32, (tm, tm), 0)
    r_in = lax.broadcasted_iota(jnp.int32, (tm, tm), 1)
    outer = r_in >> _log2(e_cnt)
    inner = r_in & (e_cnt - 1)
    return (r_out == inner * n_outer + outer).astype(BF16)


def _permute_rows(pm, x):
    if x.dtype == BF16:
        return _dot(pm, x)
    hi = x.astype(BF16)
    rem = x - hi.astype(F32)
    mid = rem.astype(BF16)
    low = (rem - mid.astype(F32)).astype(BF16)
    return _dot(pm, hi) + _dot(pm, mid) + _dot(pm, low)


def _merge_kernel(x_ref, oa_ref, ocmp_ref, oslc_ref, owin_ref, oc_ref, gts_ref,
                  wmg_ref, wbr_ref, wout_ref, g_ref, b_ref, o_ref):
    tm = x_ref.shape[0]
    x = x_ref[...]
    xb = x.astype(BF16)
    p_slc = _interleave_matrix(tm, oslc_ref.shape[0])
    p_band = _interleave_matrix(tm, owin_ref.shape[0])
    o_slc = _permute_rows(p_slc, oslc_ref[...].reshape(tm, BRANCH_WIDTH))
    o_win = _permute_rows(p_band, owin_ref[...].reshape(tm, BRANCH_WIDTH))
    o_c = _permute_rows(p_band, oc_ref[...].reshape(tm, BRANCH_WIDTH)).astype(BF16)
    gates = jax.nn.sigmoid(gts_ref[...])
    lane = lax.broadcasted_iota(jnp.int32, (tm, LANES), 1)
    lo_half = lane < HEAD_DIM
    nsa = (ocmp_ref[...], o_slc, o_win)
    ob_tiles = []
    for pair in range(NSA_HEADS // 2):
        acc = jnp.zeros((tm, LANES), F32)
        for n in range(3):
            c0 = n * NSA_HEADS + 2 * pair
            gate = jnp.where(lo_half, gates[:, c0:c0 + 1], gates[:, c0 + 1:c0 + 2])
            acc = acc + gate * nsa[n][:, pair * LANES:(pair + 1) * LANES]
        ob_tiles.append(acc.astype(BF16))
    ob = jnp.concatenate(ob_tiles, axis=1)
    branches = (oa_ref[...], ob, o_c)
    y = jnp.zeros((tm, D_MODEL), F32)
    for n in range(N_BRANCHES):
        mg = jax.nn.sigmoid(_dot(xb, wmg_ref[:, n * D_MODEL:(n + 1) * D_MODEL]))
        y = y + mg * _dot(branches[n], wbr_ref[n])
    mix = _dot(y.astype(BF16), wout_ref[...])
    o_ref[...] = _layer_norm(ALPHA * x + mix, g_ref[...], b_ref[...])


def _merge(x, o_a, o_cmp, o_slc, o_win, o_c, projf, w_mg, w_br, w_out, ln_g, ln_b, B, S,
           *, tm=512):
    T, D = x.shape
    nt = S // tm
    n_slc, n_band = S // SLC_QUERY_CHUNK, S // QUERY_BLOCK
    o_slc = o_slc.reshape(B, n_slc, SLC_QUERY_CHUNK, BRANCH_WIDTH)
    assert o_win.shape == o_c.shape == (B, n_band, QUERY_BLOCK, BRANCH_WIDTH)
    row = lambda b, a: (b * nt + a, 0)
    const2 = lambda b, a: (0, 0)
    chunked = lambda b, a: (b, 0, a, 0)
    return pl.pallas_call(
        _merge_kernel,
        grid=(B, nt),
        in_specs=[pl.BlockSpec((tm, D), row),
                  pl.BlockSpec((tm, BRANCH_WIDTH), row),
                  pl.BlockSpec((tm, BRANCH_WIDTH), row),
                  pl.BlockSpec((None, n_slc, tm // n_slc, BRANCH_WIDTH), chunked),
                  pl.BlockSpec((None, n_band, tm // n_band, BRANCH_WIDTH), chunked),
                  pl.BlockSpec((None, n_band, tm // n_band, BRANCH_WIDTH), chunked),
                  pl.BlockSpec((tm, LANES), lambda b, a: (b * nt + a, 2)),
                  pl.BlockSpec((D, N_BRANCHES * D), const2),
                  pl.BlockSpec((N_BRANCHES, BRANCH_WIDTH, D), lambda b, a: (0, 0, 0)),
                  pl.BlockSpec((D, D), const2),
                  pl.BlockSpec((1, D), const2),
                  pl.BlockSpec((1, D), const2)],
        out_specs=pl.BlockSpec((tm, D), row),
        out_shape=jax.ShapeDtypeStruct((T, D), F32),
        compiler_params=_cparams(("parallel", "parallel")),
        name="merge",
    )(x, o_a, o_cmp, o_slc, o_win, o_c, projf, w_mg, w_br, w_out, ln_g, ln_b)


def _dup_kv(w, groups):
    D = w.shape[0]
    w = w.reshape(D, 2 * groups, 1, HEAD_DIM)
    return jnp.broadcast_to(w, (D, 2 * groups, 2, HEAD_DIM)).reshape(D, 4 * groups * HEAD_DIM)


def _split_w_in(w):
    sizes = (512, 512, 512, 512, 256, 256, 256, 24, 512, 128, 128, 3072)
    offs = [0]
    for n in sizes:
        offs.append(offs[-1] + n)
    (a_q, a_k, a_v, b_q, b_kvc, b_kvs, b_kvw, b_g, c_q, c_k, c_v, m_g) = [
        w[:, offs[n]:offs[n + 1]] for n in range(len(sizes))]
    w_bf = jnp.concatenate(
        [a_q, a_k, a_v, b_q, c_q, _dup_kv(b_kvs, 2), _dup_kv(b_kvw, 2),
         _dup_kv(jnp.concatenate([c_k, c_v], axis=1), 2)], axis=1).astype(BF16)
    pad = jnp.zeros((w.shape[0], N_PROJ_F32 - 256 - 24), w.dtype)
    w_f32 = jnp.concatenate([b_kvc, b_g, pad], axis=1).astype(BF16)
    return w_bf, w_f32, m_g.astype(BF16)


def _alibi_slopes(n_heads):
    assert 8 % n_heads == 0
    return jnp.exp2(-8.0 * jnp.arange(1, n_heads + 1, dtype=F32) / n_heads)


def _token_mixing(x, B, S, w_in, diff_lam, diff_subln_g, cmp_pos, cmp_w1, cmp_w2, sinks,
                  w_branch, w_out, ln_g, ln_b, lambda_init):
    G = NSA_KV_GROUPS
    w_bf, w_f32, w_mg = _split_w_in(w_in)
    proj, projf = _proj(x, w_bf, w_f32)

    n_rows = S // CMP_STRIDE
    r = projf[:, :4 * HEAD_DIM].reshape(B, S, 2, G, HEAD_DIM).transpose(2, 0, 3, 1, 4)
    r = r.reshape(2, B * G * n_rows, CMP_STRIDE * HEAD_DIM)
    pos = cmp_pos.reshape(2, 2, CMP_STRIDE * HEAD_DIM)
    w2dup = jnp.concatenate([cmp_w2, cmp_w2], axis=-1).astype(BF16)
    kvc = _compress(r, pos, cmp_w1.astype(BF16), w2dup)

    slopes8 = _alibi_slopes(NSA_HEADS)
    o_a = _diff_attn(proj, _alibi_slopes(DIFF_HEADS), diff_lam, diff_subln_g.reshape(1, LANES),
                     B, S, lambda_init, tq=512, tk=512)
    o_cmp, sel = _cmp_attn(proj, kvc, slopes8, B, S)
    o_slc = _slc_attn(proj, slopes8, sel, B, S, tq=512, tk=512)
    o_win = _window_attn(proj, slopes8, B, S, q_blk=BLK_BQ, kv_blk=BLK_WIN, window=NSA_WINDOW,
                         out_dtype=BF16, tq=1024, tsub=128, name="nsa_win_attn")
    o_c = _window_attn(proj, _alibi_slopes(SWA_HEADS), B, S, q_blk=BLK_CQ, kv_blk=BLK_SWA,
                       window=SWA_WINDOW, sinks=sinks, out_dtype=BF16, tq=1024, tsub=128,
                       name="swa_attn")
    return _merge(x, o_a, o_cmp, o_slc, o_win, o_c, projf, w_mg, w_branch.astype(BF16),
                  w_out.astype(BF16), ln_g, ln_b, B, S)


def kernel(x, p, ffn_w_in, ffn_w_out, ln_g, ln_b, w_in, diff_lam, diff_subln_g, nsa_cmp_pos,
           nsa_cmp_w1, nsa_cmp_w2, swa_sinks, w_branch, w_out, ple_w_in, ple_w_gate):
    B, S, D = x.shape
    T = B * S
    h = x.reshape(T, D)
    ffn_in, ffn_out = ffn_w_in.astype(BF16), ffn_w_out.astype(BF16)
    ple_args = (p.reshape(DEPTH, T, PLE_DIM), ple_w_gate.astype(BF16), ple_w_in.astype(BF16))
    for i in range(DEPTH):
        lambda_init = 0.8 - 0.6 * math.exp(-0.3 * i)
        lg = ln_g[i].reshape(3, 1, D)
        lb = ln_b[i].reshape(3, 1, D)
        h = _ffn(h, ffn_in, ffn_out, (i, 0), lg[0], lb[0])
        h = _token_mixing(h, B, S, w_in[i], diff_lam[i], diff_subln_g[i], nsa_cmp_pos[i],
                          nsa_cmp_w1[i], nsa_cmp_w2[i], swa_sinks[i], w_branch[i], w_out[i],
                          lg[1], lb[1], lambda_init)
        h = _ffn(h, ffn_in, ffn_out, (i, 1), lg[2], lb[2], ple_args=ple_args)
    return h.reshape(B, S, D)
```

```python
import functools
import math

import jax
import jax.numpy as jnp
from jax import lax
from jax.experimental import pallas as pl
from jax.experimental.pallas import tpu as pltpu

F32 = jnp.float32
BF16 = jnp.bfloat16

D_MODEL = 1024
DEPTH = 2
HEAD_DIM = 64
DIFF_HEADS = 4
NSA_HEADS = 8
NSA_KV_GROUPS = 2
CMP_LEN = 32
CMP_STRIDE = 16
CMP_HIDDEN = 256
SLC_BLOCK = 64
SLC_TOPK = 16
NSA_WINDOW = 512
FORCED_SCORE = 1e4
SWA_HEADS = 8
SWA_KV_HEADS = 2
SWA_WINDOW = 128
QUERY_BLOCK = 128
SLC_QUERY_CHUNK = 32
N_BRANCHES = 3
BRANCH_WIDTH = 512
D_FF = 2816
PLE_DIM = 256
LN_EPS = 1e-5
SUBLN_EPS = 1e-5
ALPHA = (2 * DEPTH) ** 0.25
NEG_INF = -1e30
QK_SCALE = HEAD_DIM ** -0.5

LANES = 128
VMEM_LIMIT = 56 * 1024 * 1024

BLK_AQ, BLK_AK, BLK_AV = 0, 4, 8
BLK_BQ, BLK_CQ = 12, 16
BLK_SLC, BLK_WIN, BLK_SWA = 20, 24, 28
N_PROJ_BF = 32 * LANES
N_PROJ_F32 = 3 * LANES


def _cparams(sem):
    return pltpu.CompilerParams(dimension_semantics=sem, vmem_limit_bytes=VMEM_LIMIT)


def _layer_norm(h, g, b):
    mu = jnp.mean(h, axis=-1, keepdims=True)
    d = h - mu
    var = jnp.mean(d * d, axis=-1, keepdims=True)
    return d * lax.rsqrt(var + LN_EPS) * g + b


def _dot(a, b):
    return jnp.dot(a, b, preferred_element_type=F32)


def _dot_nt(a, b):
    return lax.dot_general(a, b, (((1,), (1,)), ((), ())), preferred_element_type=F32)


FF_CHUNK = 256
EPILOGUE_ROWS = 256


def _ffn_kernel(*refs, ple):
    if ple:
        x_ref, win_ref, wout_ref, g_ref, b_ref, p_ref, pwg_ref, pwi_ref, o_ref, acc_ref = refs
    else:
        x_ref, win_ref, wout_ref, g_ref, b_ref, o_ref, acc_ref = refs
    xb = x_ref[...].astype(BF16)
    for k in range(D_FF // FF_CHUNK):
        c0 = k * FF_CHUNK
        gate = _dot(xb, win_ref[:, c0:c0 + FF_CHUNK])
        up = _dot(xb, win_ref[:, D_FF + c0:D_FF + c0 + FF_CHUNK])
        hid = ((gate * jax.nn.sigmoid(gate)) * up).astype(BF16)
        contrib = _dot(hid, wout_ref[c0:c0 + FF_CHUNK, :])
        if k == 0:
            acc_ref[...] = contrib
        else:
            acc_ref[...] += contrib
    for r0 in range(0, x_ref.shape[0], EPILOGUE_ROWS):
        rows = slice(r0, r0 + EPILOGUE_ROWS)
        h = ALPHA * x_ref[rows, :] + 0.5 * acc_ref[rows, :]
        if ple:
            gp = jax.nn.sigmoid(_dot(h.astype(BF16), pwg_ref[...]))
            h = h + gp * _dot(p_ref[rows, :].astype(BF16), pwi_ref[...])
        o_ref[rows, :] = _layer_norm(h, g_ref[...], b_ref[...])


def _resident(shape, lead=()):
    block = (None,) * len(lead) + tuple(shape)
    index = tuple(lead) + (0,) * len(shape)
    return pl.BlockSpec(block, lambda i: index, pipeline_mode=pl.Buffered(1))


def _ffn(x, w_in, w_out, which, ln_g, ln_b, ple_args=None, *, tm=512):
    T, D = x.shape
    ple = ple_args is not None
    in_specs = [pl.BlockSpec((tm, D), lambda i: (i, 0)),
                _resident(w_in.shape[2:], which), _resident(w_out.shape[2:], which),
                _resident((1, D)), _resident((1, D))]
    args = [x, w_in, w_out, ln_g, ln_b]
    if ple:
        p, pwg, pwi = ple_args
        layer = which[0]
        in_specs += [pl.BlockSpec((None, tm, PLE_DIM), lambda i: (layer, i, 0)),
                     _resident(pwg.shape[1:], (layer,)), _resident(pwi.shape[1:], (layer,))]
        args += [p, pwg, pwi]
    return pl.pallas_call(
        functools.partial(_ffn_kernel, ple=ple),
        grid=(T // tm,),
        in_specs=in_specs,
        out_specs=pl.BlockSpec((tm, D), lambda i: (i, 0)),
        out_shape=jax.ShapeDtypeStruct((T, D), F32),
        scratch_shapes=[pltpu.VMEM((tm, D), F32)],
        compiler_params=_cparams(("parallel",)),
        name="ffn_ple" if ple else "ffn",
    )(*args)


PROJ_COLS = 1024


def _proj_kernel(x_ref, wb_ref, wf_ref, ob_ref, og_ref, or_ref, kv_ref):
    xb = x_ref[...].astype(BF16)
    for n0 in range(0, wb_ref.shape[1], PROJ_COLS):
        ob_ref[:, n0:n0 + PROJ_COLS] = _dot(xb, wb_ref[:, n0:n0 + PROJ_COLS]).astype(BF16)
    pf = _dot(xb, wf_ref[...])
    n_kv = 4 * HEAD_DIM
    og_ref[...] = pf[:, n_kv:]
    n_rows = x_ref.shape[0] // CMP_STRIDE
    for pair in range(n_kv // LANES):
        kv_ref[pair] = pf[:, pair * LANES:(pair + 1) * LANES]
        for l in range(CMP_STRIDE):
            tok = kv_ref[pair, pl.ds(l, n_rows, stride=CMP_STRIDE), :]
            for half in range(2):
                or_ref[2 * pair + half, :, l * HEAD_DIM:(l + 1) * HEAD_DIM] = (
                    tok[:, half * HEAD_DIM:(half + 1) * HEAD_DIM])


def _proj(x, w_bf, w_f32, *, tm=512):
    T, K = x.shape
    nb, nf = w_bf.shape[1], w_f32.shape[1]
    n_kv = 4 * HEAD_DIM
    row_w = CMP_STRIDE * HEAD_DIM
    return pl.pallas_call(
        _proj_kernel,
        grid=(T // tm,),
        in_specs=[pl.BlockSpec((tm, K), lambda i: (i, 0)),
                  pl.BlockSpec((K, nb), lambda i: (0, 0)),
                  pl.BlockSpec((K, nf), lambda i: (0, 0))],
        out_specs=[pl.BlockSpec((tm, nb), lambda i: (i, 0)),
                   pl.BlockSpec((tm, nf - n_kv), lambda i: (i, 0)),
                   pl.BlockSpec((4, tm // CMP_STRIDE, row_w), lambda i: (0, i, 0))],
        out_shape=[jax.ShapeDtypeStruct((T, nb), BF16),
                   jax.ShapeDtypeStruct((T, nf - n_kv), F32),
                   jax.ShapeDtypeStruct((4, T // CMP_STRIDE, row_w), F32)],
        scratch_shapes=[pltpu.VMEM((n_kv // LANES, tm, LANES), F32)],
        compiler_params=_cparams(("parallel",)),
        name="proj",
    )(x, w_bf, w_f32)


def _compress_kernel(r_ref, pos_ref, w1_ref, w2_ref, o_ref):
    r = r_ref[...]
    half = CMP_STRIDE * HEAD_DIM
    lo = _dot((r + pos_ref[0:1, :]).astype(BF16), w1_ref[0:half, :])
    hi = _dot((r + pos_ref[1:2, :]).astype(BF16), w1_ref[half:2 * half, :])
    pre = lo + pltpu.roll(hi, hi.shape[0] - 1, 0)
    hid = jax.nn.gelu(pre, approximate=True)
    o_ref[...] = _dot(hid.astype(BF16), w2_ref[...]).astype(o_ref.dtype)


def _compress(r, pos, w1, w2dup, *, tm=512):
    G = NSA_KV_GROUPS
    _, M, F = r.shape
    tm = min(tm, M)
    return pl.pallas_call(
        _compress_kernel,
        grid=(2 * G, M // tm),
        in_specs=[pl.BlockSpec((None, tm, F), lambda q, i: (q, i, 0)),
                  pl.BlockSpec((None, 2, F), lambda q, i: (q // G, 0, 0)),
                  pl.BlockSpec((None, 2 * F, CMP_HIDDEN), lambda q, i: (q // G, 0, 0)),
                  pl.BlockSpec((None, CMP_HIDDEN, LANES), lambda q, i: (q // G, 0, 0))],
        out_specs=pl.BlockSpec((None, tm, LANES), lambda q, i: (q, i, 0)),
        out_shape=jax.ShapeDtypeStruct((2 * G, M, LANES), BF16),
        compiler_params=_cparams(("parallel", "parallel")),
        name="nsa_compress",
    )(r, pos, w1, w2dup)


def _log2(n):
    assert n > 0 and n & (n - 1) == 0, n
    return n.bit_length() - 1


def _stack_heads(q_tiles):
    tq = q_tiles[0].shape[0]
    lane = lax.broadcasted_iota(jnp.int32, (tq, LANES), 1)
    lo = lane < HEAD_DIM
    zero = jnp.zeros((tq, LANES), BF16)
    scaled = [q * jnp.asarray(QK_SCALE, BF16) for q in q_tiles]
    parts = [jnp.where(lo, qs, zero) for qs in scaled] + [jnp.where(lo, zero, qs) for qs in scaled]
    return jnp.concatenate(parts, axis=0)


def _stacked_head_order(nh):
    return list(range(0, nh, 2)) + list(range(1, nh, 2))


def _head_column(values, tq):
    nh = len(values)
    head = lax.broadcasted_iota(jnp.int32, (nh * tq, 1), 0) >> _log2(tq)
    col = jnp.zeros((nh * tq, 1), F32)
    for h in range(nh):
        col = jnp.where(head == h, values[h], col)
    return col


KX_ONE_A, KX_COL_HI, KX_COL_LO, KX_ONE_B = 32, 33, 34, 35
POS_SPLIT = 16


def _key_extras(S, tk):
    assert S // SLC_BLOCK <= KX_ONE_A
    key = jnp.arange(S, dtype=jnp.int32)[:, None]
    lane = jnp.arange(LANES, dtype=jnp.int32)[None, :]
    col = key % tk
    x = jnp.where(lane == key // SLC_BLOCK, -NEG_INF, 0.0)
    x = jnp.where((lane == KX_ONE_A) | (lane == KX_ONE_B), 1.0, x)
    x = jnp.where(lane == KX_COL_HI, (col - col % POS_SPLIT).astype(F32), x)
    x = jnp.where(lane == KX_COL_LO, (col % POS_SPLIT).astype(F32), x)
    return x.astype(BF16)


SOFTMAX_ROWS = 32


def _attn_kernel(*refs, mode, tq, tk, use_sel, nh, lambda_init):
    refs = list(refs)
    slopes_ref = refs.pop(0)
    q_ref, k_ref, v_ref, kx_ref = refs.pop(0), refs.pop(0), refs.pop(0), refs.pop(0)
    sel_ref = refs.pop(0) if use_sel else None
    if mode == "diff":
        lam_ref, subg_ref = refs.pop(0), refs.pop(0)
    o_ref = refs.pop(0)
    mask_ref, qs_ref, s_ref, p_ref, m_ref, acc_ref, shift_ref = refs
    hg = pl.program_id(1)
    i = pl.program_id(2)
    R = nh * tq
    gqa = mode == "gqa"
    ratio = tk // tq
    half = R // 2

    xl = lax.broadcasted_iota(jnp.int32, (R, LANES), 1)

    @pl.when(i == 0)
    def _():
        if gqa:
            slope_col = _head_column([slopes_ref[hg * nh + h] for h in _stacked_head_order(nh)], tq)
        else:
            slope_col = jnp.full((R, 1), slopes_ref[hg], F32)
        shift_ref[...] = slope_col * float(tk)
        row = lax.broadcasted_iota(jnp.int32, (R, tk), 0) & (tq - 1)
        col = lax.broadcasted_iota(jnp.int32, (R, tk), 1)
        for a in range(ratio):
            mask_ref[a] = jnp.where(row - col + a * tq >= 0, 0.0, NEG_INF)
        xr = lax.broadcasted_iota(jnp.int32, (R, LANES), 0) & (tq - 1)
        xr_lo = xr & (POS_SPLIT - 1)
        qx = jnp.where(xl == KX_ONE_A, -slope_col * (xr - xr_lo).astype(F32), 0.0)
        qx = jnp.where(xl == KX_ONE_B, -slope_col * xr_lo.astype(F32), qx)
        qx = jnp.where((xl == KX_COL_HI) | (xl == KX_COL_LO), slope_col, qx)
        qs_ref[:, LANES:2 * LANES] = qx.astype(BF16)

    if gqa:
        qs_ref[:, 0:LANES] = _stack_heads([q_ref[:, p * LANES:(p + 1) * LANES]
                                           for p in range(nh // 2)])
    else:
        qs_ref[:, 0:LANES] = _stack_heads([q_ref[...]])
    if use_sel:
        n_blk = k_ref.shape[0] // SLC_BLOCK
        not_sel = jnp.concatenate([(sel_ref[...].astype(F32) - 1.0).astype(BF16)] * nh, axis=0)
        qs_ref[:, LANES:2 * LANES] = jnp.where(xl < n_blk, not_sel, qs_ref[:, LANES:2 * LANES])
    m_ref[...] = jnp.full((R, 1), NEG_INF, F32)
    acc_ref[...] = jnp.zeros(acc_ref.shape, F32)
    lo_half_k = lax.broadcasted_iota(jnp.int32, (tk, LANES), 1) < HEAD_DIM

    def scores(j, slot):
        start = pl.multiple_of(j * tk, tk)
        kk = jnp.concatenate([k_ref[pl.ds(start, tk), :], kx_ref[pl.ds(start, tk), :]], axis=1)
        for h0 in (0, half):
            s_ref[slot, h0:h0 + half, :] = _dot_nt(qs_ref[h0:h0 + half, :], kk)

    def softmax_pv(j, slot, mask_idx):
        start = pl.multiple_of(j * tk, tk)
        v = v_ref[pl.ds(start, tk), :]
        one = jnp.ones((tk, LANES), BF16)
        if gqa:
            v_halves = (jnp.where(lo_half_k, v, one), jnp.where(lo_half_k, one, v))
        else:
            v_halves = (jnp.concatenate([v, one], axis=1),) * 2
        for hh, h0 in enumerate((0, half)):
            for c in range(half // SOFTMAX_ROWS):
                r0 = h0 + c * SOFTMAX_ROWS
                rows = slice(r0, r0 + SOFTMAX_ROWS)
                s = s_ref[slot, rows, :]
                if mask_idx is not None:
                    s = s + mask_ref[mask_idx, rows, :]
                m_prev = m_ref[rows, :] - shift_ref[rows, :]
                m_new = jnp.maximum(m_prev, jnp.max(s, axis=-1, keepdims=True))
                alpha = jnp.exp(m_prev - m_new)
                p_ref[rows, :] = jnp.exp(s - m_new).astype(BF16)
                m_ref[rows, :] = m_new
                acc_ref[rows, :] = alpha * acc_ref[rows, :]
            acc_ref[h0:h0 + half, :] += _dot(p_ref[h0:h0 + half, :], v_halves[hh])

    j_last = i // ratio
    scores(0, 0)

    def pair(t, c):
        scores(2 * t + 1, 1)
        softmax_pv(2 * t, 0, None)
        scores(2 * t + 2, 0)
        softmax_pv(2 * t + 1, 1, None)
        return c

    lax.fori_loop(0, j_last // 2, pair, 0)

    @pl.when(j_last % 2 == 1)
    def _():
        scores(j_last, 1)
        softmax_pv(j_last - 1, 0, None)
        softmax_pv(j_last, 1, i % ratio)

    @pl.when(j_last % 2 == 0)
    def _():
        softmax_pv(j_last, 0, i % ratio)

    acc = acc_ref[...]
    if gqa:
        acc_e, acc_o = acc[:half], acc[half:]
        l_e = acc_e[:, HEAD_DIM:HEAD_DIM + 1]
        l_o = acc_o[:, 0:1]
        o_e = acc_e / l_e
        o_o = acc_o / l_o
        lo_half_q = lax.broadcasted_iota(jnp.int32, (tq, LANES), 1) < HEAD_DIM
        for p_ in range(nh // 2):
            pair = jnp.where(lo_half_q, o_e[p_ * tq:(p_ + 1) * tq, :], o_o[p_ * tq:(p_ + 1) * tq, :])
            o_ref[:, p_ * LANES:(p_ + 1) * LANES] = pair.astype(o_ref.dtype)
    else:
        o = acc[:, 0:LANES] / acc[:, LANES:LANES + 1]
        lam = lam_ref[...]
        lam_full = (jnp.exp(jnp.sum(lam[0:1, :] * lam[1:2, :], axis=-1, keepdims=True))
                    - jnp.exp(jnp.sum(lam[2:3, :] * lam[3:4, :], axis=-1, keepdims=True))
                    + lambda_init)
        od = o[0:tq, :] - lam_full * o[tq:2 * tq, :]
        od = od * lax.rsqrt(jnp.mean(od * od, axis=-1, keepdims=True) + SUBLN_EPS) * subg_ref[...]
        o_ref[...] = (od * (1.0 - lambda_init)).astype(o_ref.dtype)


def _attn_scratch(n_masks, R, tk, acc_width):
    return [pltpu.VMEM((n_masks, R, tk), F32),
            pltpu.VMEM((R, 2 * LANES), BF16),
            pltpu.VMEM((2, R, tk), F32),
            pltpu.VMEM((R, tk), BF16),
            pltpu.VMEM((R, 1), F32),
            pltpu.VMEM((R, acc_width), F32),
            pltpu.VMEM((R, 1), F32)]


def _window_kernel(*refs, tq, tsub, back, window, use_sink, nh, n_special):
    refs = list(refs)
    slopes_ref = refs.pop(0)
    sinks_ref = refs.pop(0) if use_sink else None
    q_ref, k_ref, v_ref, o_ref, bias_ref, s_ref, p_ref, m_ref, sink_ref = refs
    g = pl.program_id(1)
    i = pl.program_id(2)
    R = nh * tsub
    half = R // 2
    span = back + tsub
    n_sub = tq // tsub
    order = _stacked_head_order(nh)

    @pl.when(i == 0)
    def _():
        slope_col = _head_column([slopes_ref[g * nh + h] for h in order], tsub)
        if use_sink:
            sink_ref[...] = _head_column([sinks_ref[g * nh + h] for h in order], tsub)
        row = lax.broadcasted_iota(jnp.int32, (R, span), 0) & (tsub - 1)
        col = lax.broadcasted_iota(jnp.int32, (R, span), 1)
        for a in range(n_special + 1):
            dist = row - col + (a * tsub if a < n_special else back)
            ok = (dist >= 0) & (dist < window)
            bias_ref[a] = jnp.where(ok, -slope_col * dist.astype(F32), NEG_INF)

    lo_half_k = lax.broadcasted_iota(jnp.int32, (span, LANES), 1) < HEAD_DIM
    lo_half_q = lax.broadcasted_iota(jnp.int32, (tsub, LANES), 1) < HEAD_DIM
    one = jnp.ones((span, LANES), BF16)
    for sub in range(n_sub):
        it = i * n_sub + sub
        a = jnp.minimum(it, n_special)
        start = pl.multiple_of(jnp.maximum(it * tsub - back, 0), LANES)
        q_rows = slice(sub * tsub, (sub + 1) * tsub)
        qs = _stack_heads([q_ref[q_rows, p * LANES:(p + 1) * LANES] for p in range(nh // 2)])
        k = k_ref[pl.ds(start, span), :]
        v = v_ref[pl.ds(start, span), :]
        v_halves = (jnp.where(lo_half_k, v, one), jnp.where(lo_half_k, one, v))
        for h0 in (0, half):
            s_ref[sub, h0:h0 + half, :] = _dot_nt(qs[h0:h0 + half, :], k)
        accs = []
        for hh, h0 in enumerate((0, half)):
            for c in range(half // SOFTMAX_ROWS):
                r0 = h0 + c * SOFTMAX_ROWS
                rows = slice(r0, r0 + SOFTMAX_ROWS)
                s = s_ref[sub, rows, :] + bias_ref[a, rows, :]
                m = jnp.max(s, axis=-1, keepdims=True)
                p_ref[sub, rows, :] = jnp.exp(s - m).astype(BF16)
                m_ref[sub, rows, :] = m
            accs.append(_dot(p_ref[sub, h0:h0 + half, :], v_halves[hh]))

        acc_e, acc_o = accs
        l_e = acc_e[:, HEAD_DIM:HEAD_DIM + 1]
        l_o = acc_o[:, 0:1]
        if use_sink:
            extra = jnp.exp(sink_ref[...] - m_ref[sub])
            l_e = l_e + extra[:half]
            l_o = l_o + extra[half:]
        o_e = acc_e / l_e
        o_o = acc_o / l_o
        for p_ in range(nh // 2):
            pair = jnp.where(lo_half_q, o_e[p_ * tsub:(p_ + 1) * tsub, :],
                             o_o[p_ * tsub:(p_ + 1) * tsub, :])
            o_ref[sub, :, p_ * LANES:(p_ + 1) * LANES] = pair.astype(o_ref.dtype)


def _window_attn(proj, slopes, B, S, *, q_blk, kv_blk, window, sinks=None, out_dtype, tq, tsub,
                 name):
    nq = S // tq
    G, nh = 2, 4
    back = -(-window // LANES) * LANES
    n_special = -(-back // tsub)
    span = back + tsub
    n_sub = tq // tsub
    R = nh * tsub
    assert span <= S and tsub == QUERY_BLOCK
    kern = functools.partial(_window_kernel, tq=tq, tsub=tsub, back=back, window=window,
                             use_sink=sinks is not None, nh=nh, n_special=n_special)
    in_specs = [_smem_spec()]
    args = [slopes]
    if sinks is not None:
        in_specs.append(_smem_spec())
        args.append(sinks)
    in_specs += [
        pl.BlockSpec((tq, 2 * LANES), lambda b, g, i: (b * nq + i, q_blk // 2 + g)),
        pl.BlockSpec((S, LANES), lambda b, g, i: (b, kv_blk + g)),
        pl.BlockSpec((S, LANES), lambda b, g, i: (b, kv_blk + 2 + g)),
    ]
    args += [proj, proj, proj]
    return pl.pallas_call(
        kern,
        grid=(B, G, nq),
        in_specs=in_specs,
        out_specs=pl.BlockSpec((None, n_sub, tsub, 2 * LANES), lambda b, g, i: (b, i, 0, g)),
        out_shape=jax.ShapeDtypeStruct((B, S // tsub, tsub, BRANCH_WIDTH), out_dtype),
        scratch_shapes=[pltpu.VMEM((n_special + 1, R, span), F32),
                        pltpu.VMEM((n_sub, R, span), F32),
                        pltpu.VMEM((n_sub, R, span), BF16),
                        pltpu.VMEM((n_sub, R, 1), F32),
                        pltpu.VMEM((R, 1), F32)],
        compiler_params=_cparams(("parallel", "parallel", "arbitrary")),
        name=name,
    )(*args)


def _smem_spec():
    return pl.BlockSpec(memory_space=pltpu.SMEM)


def _diff_attn(proj, slopes, lam, subg, B, S, lambda_init, *, tq, tk):
    nq = S // tq
    kern = functools.partial(_attn_kernel, mode="diff", tq=tq, tk=tk, use_sel=False, nh=2,
                             lambda_init=lambda_init)
    return pl.pallas_call(
        kern,
        grid=(B, DIFF_HEADS, nq),
        in_specs=[_smem_spec(),
                  pl.BlockSpec((tq, LANES), lambda b, h, i: (b * nq + i, BLK_AQ + h)),
                  pl.BlockSpec((S, LANES), lambda b, h, i: (b, BLK_AK + h)),
                  pl.BlockSpec((S, LANES), lambda b, h, i: (b, BLK_AV + h)),
                  pl.BlockSpec((S, LANES), lambda b, h, i: (0, 0)),
                  pl.BlockSpec((4, HEAD_DIM), lambda b, h, i: (0, 0)),
                  pl.BlockSpec((1, LANES), lambda b, h, i: (0, 0))],
        out_specs=pl.BlockSpec((tq, LANES), lambda b, h, i: (b * nq + i, h)),
        out_shape=jax.ShapeDtypeStruct((B * S, BRANCH_WIDTH), BF16),
        scratch_shapes=_attn_scratch(tk // tq, 2 * tq, tk, 2 * LANES),
        compiler_params=_cparams(("parallel", "parallel", "arbitrary")),
        name="diff_attn",
    )(slopes, proj, proj, proj, _key_extras(S, tk), lam, subg)


def _slc_attn(proj, slopes, sel, B, S, *, tq, tk):
    nq = S // tq
    G, nh = 2, 4
    kern = functools.partial(_attn_kernel, mode="gqa", tq=tq, tk=tk, use_sel=True, nh=nh,
                             lambda_init=0.0)
    in_specs = [
        _smem_spec(),
        pl.BlockSpec((tq, 2 * LANES), lambda b, g, i: (b * nq + i, BLK_BQ // 2 + g)),
        pl.BlockSpec((S, LANES), lambda b, g, i: (b, BLK_SLC + g)),
        pl.BlockSpec((S, LANES), lambda b, g, i: (b, BLK_SLC + 2 + g)),
        pl.BlockSpec((S, LANES), lambda b, g, i: (0, 0)),
        pl.BlockSpec((tq, LANES), lambda b, g, i: (b * nq + i, g)),
    ]
    return pl.pallas_call(
        kern,
        grid=(B, G, nq),
        in_specs=in_specs,
        out_specs=pl.BlockSpec((tq, 2 * LANES), lambda b, g, i: (b * nq + i, g)),
        out_shape=jax.ShapeDtypeStruct((B * S, BRANCH_WIDTH), F32),
        scratch_shapes=_attn_scratch(tk // tq, nh * tq, tk, LANES),
        compiler_params=_cparams(("parallel", "parallel", "arbitrary")),
        name="nsa_slc_attn",
    )(slopes, proj, proj, proj, _key_extras(S, tk), sel)


def _cmp_attn_kernel(slopes_ref, q_ref, kc_ref, vc_ref, o_ref, sel_ref, *, tq, n_cmp, n_blk, n_sel):
    g = pl.program_id(1)
    i = pl.program_id(2)
    nh = NSA_HEADS // NSA_KV_GROUPS
    R = nh * tq
    qs = _stack_heads([q_ref[:, p * LANES:(p + 1) * LANES] for p in range(nh // 2)])
    slope_col = _head_column([slopes_ref[g * nh + h] for h in _stacked_head_order(nh)], tq)

    t = i * tq + (lax.broadcasted_iota(jnp.int32, (R, LANES), 0) & (tq - 1))
    c = lax.broadcasted_iota(jnp.int32, (R, LANES), 1)
    dist = t - (c * CMP_STRIDE + (CMP_LEN - 1))
    valid = (dist >= 0) & (c < n_cmp)
    s = _dot_nt(qs, kc_ref[...])
    s = jnp.where(valid, s - slope_col * dist.astype(F32), NEG_INF)
    m = jnp.max(s, axis=-1, keepdims=True)
    e = jnp.exp(s - m)
    p_c = e / jnp.sum(e, axis=-1, keepdims=True)
    any_valid = (t[:, 0:1] >= CMP_LEN - 1).astype(F32)
    p_c = p_c * any_valid
    o = _dot(p_c.astype(BF16), vc_ref[...])
    lane = lax.broadcasted_iota(jnp.int32, (tq, LANES), 1)
    lo_half = lane < HEAD_DIM
    for p in range(nh // 2):
        pair = jnp.where(lo_half, o[p * tq:(p + 1) * tq, :],
                         o[(nh // 2 + p) * tq:(nh // 2 + p + 1) * tq, :])
        o_ref[:, p * LANES:(p + 1) * LANES] = pair.astype(o_ref.dtype)

    p_sum = p_c[0:tq, :]
    for h in range(1, nh):
        p_sum = p_sum + p_c[h * tq:(h + 1) * tq, :]
    jj = lax.broadcasted_iota(jnp.int32, (n_blk, LANES), 0)
    cc = lax.broadcasted_iota(jnp.int32, (n_blk, LANES), 1)
    overlap = (jnp.minimum(cc * CMP_STRIDE + CMP_LEN, jj * SLC_BLOCK + SLC_BLOCK)
               - jnp.maximum(cc * CMP_STRIDE, jj * SLC_BLOCK))
    overlap = jnp.where(cc < n_cmp, jnp.maximum(overlap, 0), 0)
    w_t = (overlap.astype(F32) * (1.0 / CMP_LEN)).astype(BF16)
    hi = p_sum.astype(BF16)
    rem = p_sum - hi.astype(F32)
    mid = rem.astype(BF16)
    low = (rem - mid.astype(F32)).astype(BF16)
    score = _dot_nt(w_t, hi) + _dot_nt(w_t, mid) + _dot_nt(w_t, low)

    tpos = i * tq + lax.broadcasted_iota(jnp.int32, (n_blk, tq), 1)
    t_blk = tpos >> _log2(SLC_BLOCK)
    jb = lax.broadcasted_iota(jnp.int32, (n_blk, tq), 0)
    score = jnp.where((jb == 0) | (jb == t_blk) | (jb == t_blk - 1), FORCED_SCORE, score)
    score = jnp.where(jb > t_blk, -1.0, score)
    rank = jnp.zeros((n_blk, tq), F32)
    for kk in range(n_blk):
        row = score[kk:kk + 1, :]
        ahead = (row > score) | ((row == score) & (kk < jb))
        rank = rank + ahead.astype(F32)
    sel_t = (rank < n_sel).astype(BF16)
    sel_t = jnp.concatenate([sel_t, jnp.zeros((LANES - n_blk, tq), BF16)], axis=0)
    eye = (lax.broadcasted_iota(jnp.int32, (tq, tq), 0)
           == lax.broadcasted_iota(jnp.int32, (tq, tq), 1)).astype(BF16)
    sel_ref[...] = _dot_nt(eye, sel_t).astype(sel_ref.dtype)


def _cmp_attn(proj, kvc, slopes, B, S, *, tq=512):
    nq = S // tq
    G = NSA_KV_GROUPS
    n_cmp = (S - CMP_LEN) // CMP_STRIDE + 1
    n_blk = S // SLC_BLOCK
    n_sel = min(SLC_TOPK, n_blk)
    kern = functools.partial(_cmp_attn_kernel, tq=tq, n_cmp=n_cmp, n_blk=n_blk, n_sel=n_sel)
    return pl.pallas_call(
        kern,
        grid=(B, G, nq),
        in_specs=[_smem_spec(),
                  pl.BlockSpec((tq, 2 * LANES), lambda b, g, i: (b * nq + i, BLK_BQ // 2 + g)),
                  pl.BlockSpec((None, LANES, LANES), lambda b, g, i: (g, b, 0)),
                  pl.BlockSpec((None, LANES, LANES), lambda b, g, i: (G + g, b, 0))],
        out_specs=[pl.BlockSpec((tq, 2 * LANES), lambda b, g, i: (b * nq + i, g)),
                   pl.BlockSpec((tq, LANES), lambda b, g, i: (b * nq + i, g))],
        out_shape=[jax.ShapeDtypeStruct((B * S, BRANCH_WIDTH), F32),
                   jax.ShapeDtypeStruct((B * S, G * LANES), BF16)],
        compiler_params=_cparams(("parallel", "parallel", "parallel")),
        name="nsa_cmp_attn",
    )(slopes, proj, kvc, kvc)


def _interleave_matrix(tm, n_outer):
    e_cnt = tm // n_outer
    r_out = lax.broadcasted_iota(jnp.int32, (tm, tm), 0)
    r_in = lax.broadcasted_iota(jnp.int32, (tm, tm), 1)
    outer = r_in >> _log2(e_cnt)
    inner = r_in & (e_cnt - 1)
    return (r_out == inner * n_outer + outer).astype(BF16)


def _permute_rows(pm, x):
    if x.dtype == BF16:
        return _dot(pm, x)
    hi = x.astype(BF16)
    rem = x - hi.astype(F32)
    mid = rem.astype(BF16)
    low = (rem - mid.astype(F32)).astype(BF16)
    return _dot(pm, hi) + _dot(pm, mid) + _dot(pm, low)


def _merge_kernel(x_ref, oa_ref, ocmp_ref, oslc_ref, owin_ref, oc_ref, gts_ref,
                  wmg_ref, wbr_ref, wout_ref, g_ref, b_ref, o_ref):
    tm = x_ref.shape[0]
    x = x_ref[...]
    xb = x.astype(BF16)
    p_slc = _interleave_matrix(tm, oslc_ref.shape[0])
    p_band = _interleave_matrix(tm, owin_ref.shape[0])
    o_slc = _permute_rows(p_slc, oslc_ref[...].reshape(tm, BRANCH_WIDTH))
    o_win = _permute_rows(p_band, owin_ref[...].reshape(tm, BRANCH_WIDTH))
    o_c = _permute_rows(p_band, oc_ref[...].reshape(tm, BRANCH_WIDTH)).astype(BF16)
    gates = jax.nn.sigmoid(gts_ref[...])
    lane = lax.broadcasted_iota(jnp.int32, (tm, LANES), 1)
    lo_half = lane < HEAD_DIM
    nsa = (ocmp_ref[...], o_slc, o_win)
    ob_tiles = []
    for pair in range(NSA_HEADS // 2):
        acc = jnp.zeros((tm, LANES), F32)
        for n in range(3):
            c0 = n * NSA_HEADS + 2 * pair
            gate = jnp.where(lo_half, gates[:, c0:c0 + 1], gates[:, c0 + 1:c0 + 2])
            acc = acc + gate * nsa[n][:, pair * LANES:(pair + 1) * LANES]
        ob_tiles.append(acc.astype(BF16))
    ob = jnp.concatenate(ob_tiles, axis=1)
    branches = (oa_ref[...], ob, o_c)
    y = jnp.zeros((tm, D_MODEL), F32)
    for n in range(N_BRANCHES):
        mg = jax.nn.sigmoid(_dot(xb, wmg_ref[:, n * D_MODEL:(n + 1) * D_MODEL]))
        y = y + mg * _dot(branches[n], wbr_ref[n])
    mix = _dot(y.astype(BF16), wout_ref[...])
    o_ref[...] = _layer_norm(ALPHA * x + mix, g_ref[...], b_ref[...])


def _merge(x, o_a, o_cmp, o_slc, o_win, o_c, projf, w_mg, w_br, w_out, ln_g, ln_b, B, S,
           *, tm=512):
    T, D = x.shape
    nt = S // tm
    n_slc, n_band = S // SLC_QUERY_CHUNK, S // QUERY_BLOCK
    o_slc = o_slc.reshape(B, n_slc, SLC_QUERY_CHUNK, BRANCH_WIDTH)
    assert o_win.shape == o_c.shape == (B, n_band, QUERY_BLOCK, BRANCH_WIDTH)
    row = lambda b, a: (b * nt + a, 0)
    const2 = lambda b, a: (0, 0)
    chunked = lambda b, a: (b, 0, a, 0)
    return pl.pallas_call(
        _merge_kernel,
        grid=(B, nt),
        in_specs=[pl.BlockSpec((tm, D), row),
                  pl.BlockSpec((tm, BRANCH_WIDTH), row),
                  pl.BlockSpec((tm, BRANCH_WIDTH), row),
                  pl.BlockSpec((None, n_slc, tm // n_slc, BRANCH_WIDTH), chunked),
                  pl.BlockSpec((None, n_band, tm // n_band, BRANCH_WIDTH), chunked),
                  pl.BlockSpec((None, n_band, tm // n_band, BRANCH_WIDTH), chunked),
                  pl.BlockSpec((tm, LANES), lambda b, a: (b * nt + a, 0)),
                  pl.BlockSpec((D, N_BRANCHES * D), const2),
                  pl.BlockSpec((N_BRANCHES, BRANCH_WIDTH, D), lambda b, a: (0, 0, 0)),
                  pl.BlockSpec((D, D), const2),
                  pl.BlockSpec((1, D), const2),
                  pl.BlockSpec((1, D), const2)],
        out_specs=pl.BlockSpec((tm, D), row),
        out_shape=jax.ShapeDtypeStruct((T, D), F32),
        compiler_params=_cparams(("parallel", "parallel")),
        name="merge",
    )(x, o_a, o_cmp, o_slc, o_win, o_c, projf, w_mg, w_br, w_out, ln_g, ln_b)


def _dup_kv(w, groups):
    D = w.shape[0]
    w = w.reshape(D, 2 * groups, 1, HEAD_DIM)
    return jnp.broadcast_to(w, (D, 2 * groups, 2, HEAD_DIM)).reshape(D, 4 * groups * HEAD_DIM)


def _split_w_in(w):
    sizes = (512, 512, 512, 512, 256, 256, 256, 24, 512, 128, 128, 3072)
    offs = [0]
    for n in sizes:
        offs.append(offs[-1] + n)
    (a_q, a_k, a_v, b_q, b_kvc, b_kvs, b_kvw, b_g, c_q, c_k, c_v, m_g) = [
        w[:, offs[n]:offs[n + 1]] for n in range(len(sizes))]
    w_bf = jnp.concatenate(
        [a_q, a_k, a_v, b_q, c_q, _dup_kv(b_kvs, 2), _dup_kv(b_kvw, 2),
         _dup_kv(jnp.concatenate([c_k, c_v], axis=1), 2)], axis=1).astype(BF16)
    pad = jnp.zeros((w.shape[0], N_PROJ_F32 - 256 - 24), w.dtype)
    w_f32 = jnp.concatenate([b_kvc, b_g, pad], axis=1).astype(BF16)
    return w_bf, w_f32, m_g.astype(BF16)


def _alibi_slopes(n_heads):
    assert 8 % n_heads == 0
    return jnp.exp2(-8.0 * jnp.arange(1, n_heads + 1, dtype=F32) / n_heads)


def _token_mixing(x, B, S, w_in, diff_lam, diff_subln_g, cmp_pos, cmp_w1, cmp_w2, sinks,
                  w_branch, w_out, ln_g, ln_b, lambda_init):
    w_bf, w_f32, w_mg = _split_w_in(w_in)
    proj, gates, r = _proj(x, w_bf, w_f32)
    pos = cmp_pos.reshape(2, 2, CMP_STRIDE * HEAD_DIM)
    w2dup = jnp.concatenate([cmp_w2, cmp_w2], axis=-1).astype(BF16)
    kvc = _compress(r, pos, cmp_w1.astype(BF16), w2dup)

    slopes8 = _alibi_slopes(NSA_HEADS)
    o_a = _diff_attn(proj, _alibi_slopes(DIFF_HEADS), diff_lam, diff_subln_g.reshape(1, LANES),
                     B, S, lambda_init, tq=512, tk=512)
    o_cmp, sel = _cmp_attn(proj, kvc, slopes8, B, S)
    o_slc = _slc_attn(proj, slopes8, sel, B, S, tq=512, tk=512)
    o_win = _window_attn(proj, slopes8, B, S, q_blk=BLK_BQ, kv_blk=BLK_WIN, window=NSA_WINDOW,
                         out_dtype=BF16, tq=1024, tsub=128, name="nsa_win_attn")
    o_c = _window_attn(proj, _alibi_slopes(SWA_HEADS), B, S, q_blk=BLK_CQ, kv_blk=BLK_SWA,
                       window=SWA_WINDOW, sinks=sinks, out_dtype=BF16, tq=1024, tsub=128,
                       name="swa_attn")
    return _merge(x, o_a, o_cmp, o_slc, o_win, o_c, gates, w_mg, w_branch.astype(BF16),
                  w_out.astype(BF16), ln_g, ln_b, B, S)


def kernel(x, p, ffn_w_in, ffn_w_out, ln_g, ln_b, w_in, diff_lam, diff_subln_g, nsa_cmp_pos,
           nsa_cmp_w1, nsa_cmp_w2, swa_sinks, w_branch, w_out, ple_w_in, ple_w_gate):
    B, S, D = x.shape
    T = B * S
    h = x.reshape(T, D)
    ffn_in, ffn_out = ffn_w_in.astype(BF16), ffn_w_out.astype(BF16)
    ple_args = (p.reshape(DEPTH, T, PLE_DIM), ple_w_gate.astype(BF16), ple_w_in.astype(BF16))
    for i in range(DEPTH):
        lambda_init = 0.8 - 0.6 * math.exp(-0.3 * i)
        lg = ln_g[i].reshape(3, 1, D)
        lb = ln_b[i].reshape(3, 1, D)
        h = _ffn(h, ffn_in, ffn_out, (i, 0), lg[0], lb[0])
        h = _token_mixing(h, B, S, w_in[i], diff_lam[i], diff_subln_g[i], nsa_cmp_pos[i],
                          nsa_cmp_w1[i], nsa_cmp_w2[i], swa_sinks[i], w_branch[i], w_out[i],
                          lg[1], lb[1], lambda_init)
        h = _ffn(h, ffn_in, ffn_out, (i, 1), lg[2], lb[2], ple_args=ple_args)
    return h.reshape(B, S, D)
```

```python
import functools
import math

import jax
import jax.numpy as jnp
from jax import lax
from jax.experimental import pallas as pl
from jax.experimental.pallas import tpu as pltpu

F32 = jnp.float32
BF16 = jnp.bfloat16

D_MODEL = 1024
DEPTH = 2
HEAD_DIM = 64
DIFF_HEADS = 4
NSA_HEADS = 8
NSA_KV_GROUPS = 2
CMP_LEN = 32
CMP_STRIDE = 16
CMP_HIDDEN = 256
SLC_BLOCK = 64
SLC_TOPK = 16
NSA_WINDOW = 512
FORCED_SCORE = 1e4
SWA_HEADS = 8
SWA_KV_HEADS = 2
SWA_WINDOW = 128
QUERY_BLOCK = 128
SLC_QUERY_CHUNK = 32
N_BRANCHES = 3
BRANCH_WIDTH = 512
D_FF = 2816
PLE_DIM = 256
LN_EPS = 1e-5
SUBLN_EPS = 1e-5
ALPHA = (2 * DEPTH) ** 0.25
NEG_INF = -1e30
QK_SCALE = HEAD_DIM ** -0.5

LANES = 128
VMEM_LIMIT = 56 * 1024 * 1024

BLK_AQ, BLK_AK, BLK_AV = 0, 4, 8
BLK_BQ, BLK_CQ = 12, 16
BLK_SLC, BLK_WIN, BLK_SWA = 20, 24, 28
N_PROJ_BF = 32 * LANES
N_PROJ_F32 = 3 * LANES


def _cparams(sem):
    return pltpu.CompilerParams(dimension_semantics=sem, vmem_limit_bytes=VMEM_LIMIT)


def _layer_norm(h, g, b):
    mu = jnp.mean(h, axis=-1, keepdims=True)
    d = h - mu
    var = jnp.mean(d * d, axis=-1, keepdims=True)
    return d * lax.rsqrt(var + LN_EPS) * g + b


def _dot(a, b):
    return jnp.dot(a, b, preferred_element_type=F32)


def _dot_nt(a, b):
    return lax.dot_general(a, b, (((1,), (1,)), ((), ())), preferred_element_type=F32)


FF_CHUNK = 256
EPILOGUE_ROWS = 256


def _ffn_kernel(*refs, ple):
    if ple:
        x_ref, win_ref, wout_ref, g_ref, b_ref, p_ref, pwg_ref, pwi_ref, o_ref, acc_ref = refs
    else:
        x_ref, win_ref, wout_ref, g_ref, b_ref, o_ref, acc_ref = refs
    xb = x_ref[...].astype(BF16)
    for k in range(D_FF // FF_CHUNK):
        c0 = k * FF_CHUNK
        gate = _dot(xb, win_ref[:, c0:c0 + FF_CHUNK])
        up = _dot(xb, win_ref[:, D_FF + c0:D_FF + c0 + FF_CHUNK])
        hid = ((gate * jax.nn.sigmoid(gate)) * up).astype(BF16)
        contrib = _dot(hid, wout_ref[c0:c0 + FF_CHUNK, :])
        if k == 0:
            acc_ref[...] = contrib
        else:
            acc_ref[...] += contrib
    for r0 in range(0, x_ref.shape[0], EPILOGUE_ROWS):
        rows = slice(r0, r0 + EPILOGUE_ROWS)
        h = ALPHA * x_ref[rows, :] + 0.5 * acc_ref[rows, :]
        if ple:
            gp = jax.nn.sigmoid(_dot(h.astype(BF16), pwg_ref[...]))
            h = h + gp * _dot(p_ref[rows, :].astype(BF16), pwi_ref[...])
        o_ref[rows, :] = _layer_norm(h, g_ref[...], b_ref[...])


def _resident(shape, lead=()):
    block = (None,) * len(lead) + tuple(shape)
    index = tuple(lead) + (0,) * len(shape)
    return pl.BlockSpec(block, lambda i: index, pipeline_mode=pl.Buffered(1))


def _ffn(x, w_in, w_out, which, ln_g, ln_b, ple_args=None, *, tm=512):
    T, D = x.shape
    ple = ple_args is not None
    in_specs = [pl.BlockSpec((tm, D), lambda i: (i, 0)),
                _resident(w_in.shape[2:], which), _resident(w_out.shape[2:], which),
                _resident((1, D)), _resident((1, D))]
    args = [x, w_in, w_out, ln_g, ln_b]
    if ple:
        p, pwg, pwi = ple_args
        layer = which[0]
        in_specs += [pl.BlockSpec((None, tm, PLE_DIM), lambda i: (layer, i, 0)),
                     _resident(pwg.shape[1:], (layer,)), _resident(pwi.shape[1:], (layer,))]
        args += [p, pwg, pwi]
    return pl.pallas_call(
        functools.partial(_ffn_kernel, ple=ple),
        grid=(T // tm,),
        in_specs=in_specs,
        out_specs=pl.BlockSpec((tm, D), lambda i: (i, 0)),
        out_shape=jax.ShapeDtypeStruct((T, D), F32),
        scratch_shapes=[pltpu.VMEM((tm, D), F32)],
        compiler_params=_cparams(("parallel",)),
        name="ffn_ple" if ple else "ffn",
    )(*args)


PROJ_COLS = 1024


def _proj_kernel(x_ref, wb_ref, wf_ref, ob_ref, og_ref, or_ref, kv_ref):
    xb = x_ref[...].astype(BF16)
    for n0 in range(0, wb_ref.shape[1], PROJ_COLS):
        ob_ref[:, n0:n0 + PROJ_COLS] = _dot(xb, wb_ref[:, n0:n0 + PROJ_COLS]).astype(BF16)
    pf = _dot(xb, wf_ref[...])
    n_kv = 4 * HEAD_DIM
    og_ref[...] = pf[:, n_kv:]
    n_rows = x_ref.shape[0] // CMP_STRIDE
    for pair in range(n_kv // LANES):
        kv_ref[pair] = pf[:, pair * LANES:(pair + 1) * LANES]
        for l in range(CMP_STRIDE):
            tok = kv_ref[pair, pl.ds(l, n_rows, stride=CMP_STRIDE), :]
            for half in range(2):
                or_ref[2 * pair + half, :, l * HEAD_DIM:(l + 1) * HEAD_DIM] = (
                    tok[:, half * HEAD_DIM:(half + 1) * HEAD_DIM])


def _proj(x, w_bf, w_f32, *, tm=512):
    T, K = x.shape
    nb, nf = w_bf.shape[1], w_f32.shape[1]
    n_kv = 4 * HEAD_DIM
    row_w = CMP_STRIDE * HEAD_DIM
    return pl.pallas_call(
        _proj_kernel,
        grid=(T // tm,),
        in_specs=[pl.BlockSpec((tm, K), lambda i: (i, 0)),
                  pl.BlockSpec((K, nb), lambda i: (0, 0)),
                  pl.BlockSpec((K, nf), lambda i: (0, 0))],
        out_specs=[pl.BlockSpec((tm, nb), lambda i: (i, 0)),
                   pl.BlockSpec((tm, nf - n_kv), lambda i: (i, 0)),
                   pl.BlockSpec((4, tm // CMP_STRIDE, row_w), lambda i: (0, i, 0))],
        out_shape=[jax.ShapeDtypeStruct((T, nb), BF16),
                   jax.ShapeDtypeStruct((T, nf - n_kv), F32),
                   jax.ShapeDtypeStruct((4, T // CMP_STRIDE, row_w), F32)],
        scratch_shapes=[pltpu.VMEM((n_kv // LANES, tm, LANES), F32)],
        compiler_params=_cparams(("parallel",)),
        name="proj",
    )(x, w_bf, w_f32)


def _compress_kernel(r_ref, pos_ref, w1_ref, w2_ref, o_ref):
    r = r_ref[...]
    half = CMP_STRIDE * HEAD_DIM
    lo = _dot((r + pos_ref[0:1, :]).astype(BF16), w1_ref[0:half, :])
    hi = _dot((r + pos_ref[1:2, :]).astype(BF16), w1_ref[half:2 * half, :])
    pre = lo + pltpu.roll(hi, hi.shape[0] - 1, 0)
    hid = jax.nn.gelu(pre, approximate=True)
    o_ref[...] = _dot(hid.astype(BF16), w2_ref[...]).astype(o_ref.dtype)


def _compress(r, pos, w1, w2dup, *, tm=512):
    G = NSA_KV_GROUPS
    _, M, F = r.shape
    tm = min(tm, M)
    return pl.pallas_call(
        _compress_kernel,
        grid=(2 * G, M // tm),
        in_specs=[pl.BlockSpec((None, tm, F), lambda q, i: (q, i, 0)),
                  pl.BlockSpec((None, 2, F), lambda q, i: (q // G, 0, 0)),
                  pl.BlockSpec((None, 2 * F, CMP_HIDDEN), lambda q, i: (q // G, 0, 0)),
                  pl.BlockSpec((None, CMP_HIDDEN, LANES), lambda q, i: (q // G, 0, 0))],
        out_specs=pl.BlockSpec((None, tm, LANES), lambda q, i: (q, i, 0)),
        out_shape=jax.ShapeDtypeStruct((2 * G, M, LANES), BF16),
        compiler_params=_cparams(("parallel", "parallel")),
        name="nsa_compress",
    )(r, pos, w1, w2dup)


def _log2(n):
    assert n > 0 and n & (n - 1) == 0, n
    return n.bit_length() - 1


def _stack_heads(q_tiles):
    tq = q_tiles[0].shape[0]
    lane = lax.broadcasted_iota(jnp.int32, (tq, LANES), 1)
    lo = lane < HEAD_DIM
    zero = jnp.zeros((tq, LANES), BF16)
    scaled = [q * jnp.asarray(QK_SCALE, BF16) for q in q_tiles]
    parts = [jnp.where(lo, qs, zero) for qs in scaled] + [jnp.where(lo, zero, qs) for qs in scaled]
    return jnp.concatenate(parts, axis=0)


def _stacked_head_order(nh):
    return list(range(0, nh, 2)) + list(range(1, nh, 2))


def _head_column(values, tq):
    nh = len(values)
    head = lax.broadcasted_iota(jnp.int32, (nh * tq, 1), 0) >> _log2(tq)
    col = jnp.zeros((nh * tq, 1), F32)
    for h in range(nh):
        col = jnp.where(head == h, values[h], col)
    return col


KX_ONE_A, KX_COL_HI, KX_COL_LO, KX_ONE_B = 32, 33, 34, 35
POS_SPLIT = 16


def _key_extras(S, tk):
    assert S // SLC_BLOCK <= KX_ONE_A
    key = jnp.arange(S, dtype=jnp.int32)[:, None]
    lane = jnp.arange(LANES, dtype=jnp.int32)[None, :]
    col = key % tk
    x = jnp.where(lane == key // SLC_BLOCK, -NEG_INF, 0.0)
    x = jnp.where((lane == KX_ONE_A) | (lane == KX_ONE_B), 1.0, x)
    x = jnp.where(lane == KX_COL_HI, (col - col % POS_SPLIT).astype(F32), x)
    x = jnp.where(lane == KX_COL_LO, (col % POS_SPLIT).astype(F32), x)
    return x.astype(BF16)


SOFTMAX_ROWS = 32


def _attn_kernel(*refs, mode, tq, tk, nq, use_sel, nh, lambda_init):
    refs = list(refs)
    slopes_ref = refs.pop(0)
    q_ref, k_ref, v_ref, kx_ref = refs.pop(0), refs.pop(0), refs.pop(0), refs.pop(0)
    sel_ref = refs.pop(0) if use_sel else None
    if mode == "diff":
        lam_ref, subg_ref = refs.pop(0), refs.pop(0)
    o_ref = refs.pop(0)
    mask_ref, qs_ref, s_ref, p_ref, m_ref, acc_ref, shift_ref = refs
    hg = pl.program_id(1)
    R = nh * tq
    gqa = mode == "gqa"
    ratio = tk // tq
    half = R // 2

    xl = lax.broadcasted_iota(jnp.int32, (R, LANES), 1)

    if gqa:
        slope_col = _head_column([slopes_ref[hg * nh + h] for h in _stacked_head_order(nh)], tq)
    else:
        slope_col = jnp.full((R, 1), slopes_ref[hg], F32)
    shift_ref[...] = slope_col * float(tk)
    row = lax.broadcasted_iota(jnp.int32, (R, tk), 0) & (tq - 1)
    col = lax.broadcasted_iota(jnp.int32, (R, tk), 1)
    for a in range(ratio):
        mask_ref[a] = jnp.where(row - col + a * tq >= 0, 0.0, NEG_INF)
    xr = lax.broadcasted_iota(jnp.int32, (R, LANES), 0) & (tq - 1)
    xr_lo = xr & (POS_SPLIT - 1)
    qx = jnp.where(xl == KX_ONE_A, -slope_col * (xr - xr_lo).astype(F32), 0.0)
    qx = jnp.where(xl == KX_ONE_B, -slope_col * xr_lo.astype(F32), qx)
    qx = jnp.where((xl == KX_COL_HI) | (xl == KX_COL_LO), slope_col, qx)
    qs_ref[:, LANES:2 * LANES] = qx.astype(BF16)

    def tile(i, carry):
        rows_q = pl.ds(pl.multiple_of(i * tq, tq), tq)
        if gqa:
            qs_ref[:, 0:LANES] = _stack_heads([q_ref[rows_q, p * LANES:(p + 1) * LANES]
                                               for p in range(nh // 2)])
        else:
            qs_ref[:, 0:LANES] = _stack_heads([q_ref[rows_q, :]])
        if use_sel:
            n_blk = k_ref.shape[0] // SLC_BLOCK
            not_sel = jnp.concatenate([(sel_ref[rows_q, :].astype(F32) - 1.0).astype(BF16)] * nh, axis=0)
            qs_ref[:, LANES:2 * LANES] = jnp.where(xl < n_blk, not_sel, qs_ref[:, LANES:2 * LANES])
        m_ref[...] = jnp.full((R, 1), NEG_INF, F32)
        acc_ref[...] = jnp.zeros(acc_ref.shape, F32)
        lo_half_k = lax.broadcasted_iota(jnp.int32, (tk, LANES), 1) < HEAD_DIM

        def scores(j, slot):
            start = pl.multiple_of(j * tk, tk)
            kk = jnp.concatenate([k_ref[pl.ds(start, tk), :], kx_ref[pl.ds(start, tk), :]], axis=1)
            for h0 in (0, half):
                s_ref[slot, h0:h0 + half, :] = _dot_nt(qs_ref[h0:h0 + half, :], kk)

        def softmax_pv(j, slot, mask_idx):
            start = pl.multiple_of(j * tk, tk)
            v = v_ref[pl.ds(start, tk), :]
            one = jnp.ones((tk, LANES), BF16)
            if gqa:
                v_halves = (jnp.where(lo_half_k, v, one), jnp.where(lo_half_k, one, v))
            else:
                v_halves = (jnp.concatenate([v, one], axis=1),) * 2
            for hh, h0 in enumerate((0, half)):
                for c in range(half // SOFTMAX_ROWS):
                    r0 = h0 + c * SOFTMAX_ROWS
                    rows = slice(r0, r0 + SOFTMAX_ROWS)
                    s = s_ref[slot, rows, :]
                    if mask_idx is not None:
                        s = s + mask_ref[mask_idx, rows, :]
                    m_prev = m_ref[rows, :] - shift_ref[rows, :]
                    m_new = jnp.maximum(m_prev, jnp.max(s, axis=-1, keepdims=True))
                    alpha = jnp.exp(m_prev - m_new)
                    p_ref[rows, :] = jnp.exp(s - m_new).astype(BF16)
                    m_ref[rows, :] = m_new
                    acc_ref[rows, :] = alpha * acc_ref[rows, :]
                acc_ref[h0:h0 + half, :] += _dot(p_ref[h0:h0 + half, :], v_halves[hh])

        j_last = i // ratio
        scores(0, 0)

        def pair(t, c):
            scores(2 * t + 1, 1)
            softmax_pv(2 * t, 0, None)
            scores(2 * t + 2, 0)
            softmax_pv(2 * t + 1, 1, None)
            return c

        lax.fori_loop(0, j_last // 2, pair, 0)

        @pl.when(j_last % 2 == 1)
        def _():
            scores(j_last, 1)
            softmax_pv(j_last - 1, 0, None)
            softmax_pv(j_last, 1, i % ratio)

        @pl.when(j_last % 2 == 0)
        def _():
            softmax_pv(j_last, 0, i % ratio)

        acc = acc_ref[...]
        if gqa:
            acc_e, acc_o = acc[:half], acc[half:]
            l_e = acc_e[:, HEAD_DIM:HEAD_DIM + 1]
            l_o = acc_o[:, 0:1]
            o_e = acc_e / l_e
            o_o = acc_o / l_o
            lo_half_q = lax.broadcasted_iota(jnp.int32, (tq, LANES), 1) < HEAD_DIM
            for p_ in range(nh // 2):
                pair = jnp.where(lo_half_q, o_e[p_ * tq:(p_ + 1) * tq, :], o_o[p_ * tq:(p_ + 1) * tq, :])
                o_ref[rows_q, p_ * LANES:(p_ + 1) * LANES] = pair.astype(o_ref.dtype)
        else:
            o = acc[:, 0:LANES] / acc[:, LANES:LANES + 1]
            lam = lam_ref[...]
            lam_full = (jnp.exp(jnp.sum(lam[0:1, :] * lam[1:2, :], axis=-1, keepdims=True))
                        - jnp.exp(jnp.sum(lam[2:3, :] * lam[3:4, :], axis=-1, keepdims=True))
                        + lambda_init)
            od = o[0:tq, :] - lam_full * o[tq:2 * tq, :]
            od = od * lax.rsqrt(jnp.mean(od * od, axis=-1, keepdims=True) + SUBLN_EPS) * subg_ref[...]
            o_ref[rows_q, :] = (od * (1.0 - lambda_init)).astype(o_ref.dtype)
        return carry

    lax.fori_loop(0, nq, tile, 0)


def _attn_scratch(n_masks, R, tk, acc_width):
    return [pltpu.VMEM((n_masks, R, tk), F32),
            pltpu.VMEM((R, 2 * LANES), BF16),
            pltpu.VMEM((2, R, tk), F32),
            pltpu.VMEM((R, tk), BF16),
            pltpu.VMEM((R, 1), F32),
            pltpu.VMEM((R, acc_width), F32),
            pltpu.VMEM((R, 1), F32)]


def _window_kernel(*refs, tq, tsub, back, window, use_sink, nh, n_special):
    refs = list(refs)
    slopes_ref = refs.pop(0)
    sinks_ref = refs.pop(0) if use_sink else None
    q_ref, k_ref, v_ref, o_ref, bias_ref, s_ref, p_ref, m_ref, sink_ref = refs
    g = pl.program_id(1)
    i = pl.program_id(2)
    R = nh * tsub
    half = R // 2
    span = back + tsub
    n_sub = tq // tsub
    order = _stacked_head_order(nh)

    @pl.when(i == 0)
    def _():
        slope_col = _head_column([slopes_ref[g * nh + h] for h in order], tsub)
        if use_sink:
            sink_ref[...] = _head_column([sinks_ref[g * nh + h] for h in order], tsub)
        row = lax.broadcasted_iota(jnp.int32, (R, span), 0) & (tsub - 1)
        col = lax.broadcasted_iota(jnp.int32, (R, span), 1)
        for a in range(n_special + 1):
            dist = row - col + (a * tsub if a < n_special else back)
            ok = (dist >= 0) & (dist < window)
            bias_ref[a] = jnp.where(ok, -slope_col * dist.astype(F32), NEG_INF)

    lo_half_k = lax.broadcasted_iota(jnp.int32, (span, LANES), 1) < HEAD_DIM
    lo_half_q = lax.broadcasted_iota(jnp.int32, (tsub, LANES), 1) < HEAD_DIM
    one = jnp.ones((span, LANES), BF16)
    for sub in range(n_sub):
        it = i * n_sub + sub
        a = jnp.minimum(it, n_special)
        start = pl.multiple_of(jnp.maximum(it * tsub - back, 0), LANES)
        q_rows = slice(sub * tsub, (sub + 1) * tsub)
        qs = _stack_heads([q_ref[q_rows, p * LANES:(p + 1) * LANES] for p in range(nh // 2)])
        k = k_ref[pl.ds(start, span), :]
        v = v_ref[pl.ds(start, span), :]
        v_halves = (jnp.where(lo_half_k, v, one), jnp.where(lo_half_k, one, v))
        for h0 in (0, half):
            s_ref[sub, h0:h0 + half, :] = _dot_nt(qs[h0:h0 + half, :], k)
        accs = []
        for hh, h0 in enumerate((0, half)):
            for c in range(half // SOFTMAX_ROWS):
                r0 = h0 + c * SOFTMAX_ROWS
                rows = slice(r0, r0 + SOFTMAX_ROWS)
                s = s_ref[sub, rows, :] + bias_ref[a, rows, :]
                m = jnp.max(s, axis=-1, keepdims=True)
                p_ref[sub, rows, :] = jnp.exp(s - m).astype(BF16)
                m_ref[sub, rows, :] = m
            accs.append(_dot(p_ref[sub, h0:h0 + half, :], v_halves[hh]))

        acc_e, acc_o = accs
        l_e = acc_e[:, HEAD_DIM:HEAD_DIM + 1]
        l_o = acc_o[:, 0:1]
        if use_sink:
            extra = jnp.exp(sink_ref[...] - m_ref[sub])
            l_e = l_e + extra[:half]
            l_o = l_o + extra[half:]
        o_e = acc_e / l_e
        o_o = acc_o / l_o
        for p_ in range(nh // 2):
            pair = jnp.where(lo_half_q, o_e[p_ * tsub:(p_ + 1) * tsub, :],
                             o_o[p_ * tsub:(p_ + 1) * tsub, :])
            o_ref[sub, :, p_ * LANES:(p_ + 1) * LANES] = pair.astype(o_ref.dtype)


def _window_attn(proj, slopes, B, S, *, q_blk, kv_blk, window, sinks=None, out_dtype, tq, tsub,
                 name):
    nq = S // tq
    G, nh = 2, 4
    back = -(-window // LANES) * LANES
    n_special = -(-back // tsub)
    span = back + tsub
    n_sub = tq // tsub
    R = nh * tsub
    assert span <= S and tsub == QUERY_BLOCK
    kern = functools.partial(_window_kernel, tq=tq, tsub=tsub, back=back, window=window,
                             use_sink=sinks is not None, nh=nh, n_special=n_special)
    in_specs = [_smem_spec()]
    args = [slopes]
    if sinks is not None:
        in_specs.append(_smem_spec())
        args.append(sinks)
    in_specs += [
        pl.BlockSpec((tq, 2 * LANES), lambda b, g, i: (b * nq + i, q_blk // 2 + g)),
        pl.BlockSpec((S, LANES), lambda b, g, i: (b, kv_blk + g)),
        pl.BlockSpec((S, LANES), lambda b, g, i: (b, kv_blk + 2 + g)),
    ]
    args += [proj, proj, proj]
    return pl.pallas_call(
        kern,
        grid=(B, G, nq),
        in_specs=in_specs,
        out_specs=pl.BlockSpec((None, n_sub, tsub, 2 * LANES), lambda b, g, i: (b, i, 0, g)),
        out_shape=jax.ShapeDtypeStruct((B, S // tsub, tsub, BRANCH_WIDTH), out_dtype),
        scratch_shapes=[pltpu.VMEM((n_special + 1, R, span), F32),
                        pltpu.VMEM((n_sub, R, span), F32),
                        pltpu.VMEM((n_sub, R, span), BF16),
                        pltpu.VMEM((n_sub, R, 1), F32),
                        pltpu.VMEM((R, 1), F32)],
        compiler_params=_cparams(("parallel", "parallel", "arbitrary")),
        name=name,
    )(*args)


def _smem_spec():
    return pl.BlockSpec(memory_space=pltpu.SMEM)


def _diff_attn(proj, slopes, lam, subg, B, S, lambda_init, *, tq, tk):
    kern = functools.partial(_attn_kernel, mode="diff", tq=tq, tk=tk, nq=S // tq, use_sel=False,
                             nh=2, lambda_init=lambda_init)
    return pl.pallas_call(
        kern,
        grid=(B, DIFF_HEADS),
        in_specs=[_smem_spec(),
                  pl.BlockSpec((S, LANES), lambda b, h: (b, BLK_AQ + h)),
                  pl.BlockSpec((S, LANES), lambda b, h: (b, BLK_AK + h)),
                  pl.BlockSpec((S, LANES), lambda b, h: (b, BLK_AV + h)),
                  pl.BlockSpec((S, LANES), lambda b, h: (0, 0)),
                  pl.BlockSpec((4, HEAD_DIM), lambda b, h: (0, 0)),
                  pl.BlockSpec((1, LANES), lambda b, h: (0, 0))],
        out_specs=pl.BlockSpec((S, LANES), lambda b, h: (b, h)),
        out_shape=jax.ShapeDtypeStruct((B * S, BRANCH_WIDTH), BF16),
        scratch_shapes=_attn_scratch(tk // tq, 2 * tq, tk, 2 * LANES),
        compiler_params=_cparams(("parallel", "parallel")),
        name="diff_attn",
    )(slopes, proj, proj, proj, _key_extras(S, tk), lam, subg)


def _slc_attn(proj, slopes, sel, B, S, *, tq, tk):
    G, nh = 2, 4
    kern = functools.partial(_attn_kernel, mode="gqa", tq=tq, tk=tk, nq=S // tq, use_sel=True,
                             nh=nh, lambda_init=0.0)
    in_specs = [
        _smem_spec(),
        pl.BlockSpec((S, 2 * LANES), lambda b, g: (b, BLK_BQ // 2 + g)),
        pl.BlockSpec((S, LANES), lambda b, g: (b, BLK_SLC + g)),
        pl.BlockSpec((S, LANES), lambda b, g: (b, BLK_SLC + 2 + g)),
        pl.BlockSpec((S, LANES), lambda b, g: (0, 0)),
        pl.BlockSpec((S, LANES), lambda b, g: (b, g)),
    ]
    return pl.pallas_call(
        kern,
        grid=(B, G),
        in_specs=in_specs,
        out_specs=pl.BlockSpec((S, 2 * LANES), lambda b, g: (b, g)),
        out_shape=jax.ShapeDtypeStruct((B * S, BRANCH_WIDTH), F32),
        scratch_shapes=_attn_scratch(tk // tq, nh * tq, tk, LANES),
        compiler_params=_cparams(("parallel", "parallel")),
        name="nsa_slc_attn",
    )(slopes, proj, proj, proj, _key_extras(S, tk), sel)


def _cmp_attn_kernel(slopes_ref, q_ref, kc_ref, vc_ref, o_ref, sel_ref, *, tq, n_cmp, n_blk, n_sel):
    g = pl.program_id(1)
    i = pl.program_id(2)
    nh = NSA_HEADS // NSA_KV_GROUPS
    R = nh * tq
    qs = _stack_heads([q_ref[:, p * LANES:(p + 1) * LANES] for p in range(nh // 2)])
    slope_col = _head_column([slopes_ref[g * nh + h] for h in _stacked_head_order(nh)], tq)

    t = i * tq + (lax.broadcasted_iota(jnp.int32, (R, LANES), 0) & (tq - 1))
    c = lax.broadcasted_iota(jnp.int32, (R, LANES), 1)
    dist = t - (c * CMP_STRIDE + (CMP_LEN - 1))
    valid = (dist >= 0) & (c < n_cmp)
    s = _dot_nt(qs, kc_ref[...])
    s = jnp.where(valid, s - slope_col * dist.astype(F32), NEG_INF)
    m = jnp.max(s, axis=-1, keepdims=True)
    e = jnp.exp(s - m)
    p_c = e / jnp.sum(e, axis=-1, keepdims=True)
    any_valid = (t[:, 0:1] >= CMP_LEN - 1).astype(F32)
    p_c = p_c * any_valid
    o = _dot(p_c.astype(BF16), vc_ref[...])
    lane = lax.broadcasted_iota(jnp.int32, (tq, LANES), 1)
    lo_half = lane < HEAD_DIM
    for p in range(nh // 2):
        pair = jnp.where(lo_half, o[p * tq:(p + 1) * tq, :],
                         o[(nh // 2 + p) * tq:(nh // 2 + p + 1) * tq, :])
        o_ref[:, p * LANES:(p + 1) * LANES] = pair.astype(o_ref.dtype)

    p_sum = p_c[0:tq, :]
    for h in range(1, nh):
        p_sum = p_sum + p_c[h * tq:(h + 1) * tq, :]
    jj = lax.broadcasted_iota(jnp.int32, (n_blk, LANES), 0)
    cc = lax.broadcasted_iota(jnp.int32, (n_blk, LANES), 1)
    overlap = (jnp.minimum(cc * CMP_STRIDE + CMP_LEN, jj * SLC_BLOCK + SLC_BLOCK)
               - jnp.maximum(cc * CMP_STRIDE, jj * SLC_BLOCK))
    overlap = jnp.where(cc < n_cmp, jnp.maximum(overlap, 0), 0)
    w_t = (overlap.astype(F32) * (1.0 / CMP_LEN)).astype(BF16)
    hi = p_sum.astype(BF16)
    rem = p_sum - hi.astype(F32)
    mid = rem.astype(BF16)
    low = (rem - mid.astype(F32)).astype(BF16)
    score = _dot_nt(w_t, hi) + _dot_nt(w_t, mid) + _dot_nt(w_t, low)

    tpos = i * tq + lax.broadcasted_iota(jnp.int32, (n_blk, tq), 1)
    t_blk = tpos >> _log2(SLC_BLOCK)
    jb = lax.broadcasted_iota(jnp.int32, (n_blk, tq), 0)
    score = jnp.where((jb == 0) | (jb == t_blk) | (jb == t_blk - 1), FORCED_SCORE, score)
    score = jnp.where(jb > t_blk, -1.0, score)
    rank = jnp.zeros((n_blk, tq), F32)
    for kk in range(n_blk):
        row = score[kk:kk + 1, :]
        ahead = (row > score) | ((row == score) & (kk < jb))
        rank = rank + ahead.astype(F32)
    sel_t = (rank < n_sel).astype(BF16)
    sel_t = jnp.concatenate([sel_t, jnp.zeros((LANES - n_blk, tq), BF16)], axis=0)
    eye = (lax.broadcasted_iota(jnp.int32, (tq, tq), 0)
           == lax.broadcasted_iota(jnp.int32, (tq, tq), 1)).astype(BF16)
    sel_ref[...] = _dot_nt(eye, sel_t).astype(sel_ref.dtype)


def _cmp_attn(proj, kvc, slopes, B, S, *, tq=512):
    nq = S // tq
    G = NSA_KV_GROUPS
    n_cmp = (S - CMP_LEN) // CMP_STRIDE + 1
    n_blk = S // SLC_BLOCK
    n_sel = min(SLC_TOPK, n_blk)
    kern = functools.partial(_cmp_attn_kernel, tq=tq, n_cmp=n_cmp, n_blk=n_blk, n_sel=n_sel)
    return pl.pallas_call(
        kern,
        grid=(B, G, nq),
        in_specs=[_smem_spec(),
                  pl.BlockSpec((tq, 2 * LANES), lambda b, g, i: (b * nq + i, BLK_BQ // 2 + g)),
                  pl.BlockSpec((None, LANES, LANES), lambda b, g, i: (g, b, 0)),
                  pl.BlockSpec((None, LANES, LANES), lambda b, g, i: (G + g, b, 0))],
        out_specs=[pl.BlockSpec((tq, 2 * LANES), lambda b, g, i: (b * nq + i, g)),
                   pl.BlockSpec((tq, LANES), lambda b, g, i: (b * nq + i, g))],
        out_shape=[jax.ShapeDtypeStruct((B * S, BRANCH_WIDTH), F32),
                   jax.ShapeDtypeStruct((B * S, G * LANES), BF16)],
        compiler_params=_cparams(("parallel", "parallel", "parallel")),
        name="nsa_cmp_attn",
    )(slopes, proj, kvc, kvc)


def _interleave_matrix(tm, n_outer):
    e_cnt = tm // n_outer
    r_out = lax.broadcasted_iota(jnp.int32, (tm, tm), 0)
    r_in = lax.broadcasted_iota(jnp.int32, (tm, tm), 1)
    outer = r_in >> _log2(e_cnt)
    inner = r_in & (e_cnt - 1)
    return (r_out == inner * n_outer + outer).astype(BF16)


def _permute_rows(pm, x):
    if x.dtype == BF16:
        return _dot(pm, x)
    hi = x.astype(BF16)
    rem = x - hi.astype(F32)
    mid = rem.astype(BF16)
    low = (rem - mid.astype(F32)).astype(BF16)
    return _dot(pm, hi) + _dot(pm, mid) + _dot(pm, low)


def _merge_kernel(x_ref, oa_ref, ocmp_ref, oslc_ref, owin_ref, oc_ref, gts_ref,
                  wmg_ref, wbr_ref, wout_ref, g_ref, b_ref, o_ref):
    tm = x_ref.shape[0]
    x = x_ref[...]
    xb = x.astype(BF16)
    p_slc = _interleave_matrix(tm, oslc_ref.shape[0])
    p_band = _interleave_matrix(tm, owin_ref.shape[0])
    o_slc = _permute_rows(p_slc, oslc_ref[...].reshape(tm, BRANCH_WIDTH))
    o_win = _permute_rows(p_band, owin_ref[...].reshape(tm, BRANCH_WIDTH))
    o_c = _permute_rows(p_band, oc_ref[...].reshape(tm, BRANCH_WIDTH)).astype(BF16)
    gates = jax.nn.sigmoid(gts_ref[...])
    lane = lax.broadcasted_iota(jnp.int32, (tm, LANES), 1)
    lo_half = lane < HEAD_DIM
    nsa = (ocmp_ref[...], o_slc, o_win)
    ob_tiles = []
    for pair in range(NSA_HEADS // 2):
        acc = jnp.zeros((tm, LANES), F32)
        for n in range(3):
            c0 = n * NSA_HEADS + 2 * pair
            gate = jnp.where(lo_half, gates[:, c0:c0 + 1], gates[:, c0 + 1:c0 + 2])
            acc = acc + gate * nsa[n][:, pair * LANES:(pair + 1) * LANES]
        ob_tiles.append(acc.astype(BF16))
    ob = jnp.concatenate(ob_tiles, axis=1)
    branches = (oa_ref[...], ob, o_c)
    y = jnp.zeros((tm, D_MODEL), F32)
    for n in range(N_BRANCHES):
        mg = jax.nn.sigmoid(_dot(xb, wmg_ref[:, n * D_MODEL:(n + 1) * D_MODEL]))
        y = y + mg * _dot(branches[n], wbr_ref[n])
    mix = _dot(y.astype(BF16), wout_ref[...])
    o_ref[...] = _layer_norm(ALPHA * x + mix, g_ref[...], b_ref[...])


def _merge(x, o_a, o_cmp, o_slc, o_win, o_c, projf, w_mg, w_br, w_out, ln_g, ln_b, B, S,
           *, tm=512):
    T, D = x.shape
    nt = S // tm
    n_slc, n_band = S // SLC_QUERY_CHUNK, S // QUERY_BLOCK
    o_slc = o_slc.reshape(B, n_slc, SLC_QUERY_CHUNK, BRANCH_WIDTH)
    assert o_win.shape == o_c.shape == (B, n_band, QUERY_BLOCK, BRANCH_WIDTH)
    row = lambda b, a: (b * nt + a, 0)
    const2 = lambda b, a: (0, 0)
    chunked = lambda b, a: (b, 0, a, 0)
    return pl.pallas_call(
        _merge_kernel,
        grid=(B, nt),
        in_specs=[pl.BlockSpec((tm, D), row),
                  pl.BlockSpec((tm, BRANCH_WIDTH), row),
                  pl.BlockSpec((tm, BRANCH_WIDTH), row),
                  pl.BlockSpec((None, n_slc, tm // n_slc, BRANCH_WIDTH), chunked),
                  pl.BlockSpec((None, n_band, tm // n_band, BRANCH_WIDTH), chunked),
                  pl.BlockSpec((None, n_band, tm // n_band, BRANCH_WIDTH), chunked),
                  pl.BlockSpec((tm, LANES), lambda b, a: (b * nt + a, 0)),
                  pl.BlockSpec((D, N_BRANCHES * D), const2),
                  pl.BlockSpec((N_BRANCHES, BRANCH_WIDTH, D), lambda b, a: (0, 0, 0)),
                  pl.BlockSpec((D, D), const2),
                  pl.BlockSpec((1, D), const2),
                  pl.BlockSpec((1, D), const2)],
        out_specs=pl.BlockSpec((tm, D), row),
        out_shape=jax.ShapeDtypeStruct((T, D), F32),
        compiler_params=_cparams(("parallel", "parallel")),
        name="merge",
    )(x, o_a, o_cmp, o_slc, o_win, o_c, projf, w_mg, w_br, w_out, ln_g, ln_b)


def _dup_kv(w, groups):
    D = w.shape[0]
    w = w.reshape(D, 2 * groups, 1, HEAD_DIM)
    return jnp.broadcast_to(w, (D, 2 * groups, 2, HEAD_DIM)).reshape(D, 4 * groups * HEAD_DIM)


def _split_w_in(w):
    sizes = (512, 512, 512, 512, 256, 256, 256, 24, 512, 128, 128, 3072)
    offs = [0]
    for n in sizes:
        offs.append(offs[-1] + n)
    (a_q, a_k, a_v, b_q, b_kvc, b_kvs, b_kvw, b_g, c_q, c_k, c_v, m_g) = [
        w[:, offs[n]:offs[n + 1]] for n in range(len(sizes))]
    w_bf = jnp.concatenate(
        [a_q, a_k, a_v, b_q, c_q, _dup_kv(b_kvs, 2), _dup_kv(b_kvw, 2),
         _dup_kv(jnp.concatenate([c_k, c_v], axis=1), 2)], axis=1).astype(BF16)
    pad = jnp.zeros((w.shape[0], N_PROJ_F32 - 256 - 24), w.dtype)
    w_f32 = jnp.concatenate([b_kvc, b_g, pad], axis=1).astype(BF16)
    return w_bf, w_f32, m_g.astype(BF16)


def _alibi_slopes(n_heads):
    assert 8 % n_heads == 0
    return jnp.exp2(-8.0 * jnp.arange(1, n_heads + 1, dtype=F32) / n_heads)


def _token_mixing(x, B, S, w_in, diff_lam, diff_subln_g, cmp_pos, cmp_w1, cmp_w2, sinks,
                  w_branch, w_out, ln_g, ln_b, lambda_init):
    w_bf, w_f32, w_mg = _split_w_in(w_in)
    proj, gates, r = _proj(x, w_bf, w_f32)
    pos = cmp_pos.reshape(2, 2, CMP_STRIDE * HEAD_DIM)
    w2dup = jnp.concatenate([cmp_w2, cmp_w2], axis=-1).astype(BF16)
    kvc = _compress(r, pos, cmp_w1.astype(BF16), w2dup)

    slopes8 = _alibi_slopes(NSA_HEADS)
    o_a = _diff_attn(proj, _alibi_slopes(DIFF_HEADS), diff_lam, diff_subln_g.reshape(1, LANES),
                     B, S, lambda_init, tq=512, tk=512)
    o_cmp, sel = _cmp_attn(proj, kvc, slopes8, B, S)
    o_slc = _slc_attn(proj, slopes8, sel, B, S, tq=512, tk=512)
    o_win = _window_attn(proj, slopes8, B, S, q_blk=BLK_BQ, kv_blk=BLK_WIN, window=NSA_WINDOW,
                         out_dtype=BF16, tq=1024, tsub=128, name="nsa_win_attn")
    o_c = _window_attn(proj, _alibi_slopes(SWA_HEADS), B, S, q_blk=BLK_CQ, kv_blk=BLK_SWA,
                       window=SWA_WINDOW, sinks=sinks, out_dtype=BF16, tq=1024, tsub=128,
                       name="swa_attn")
    return _merge(x, o_a, o_cmp, o_slc, o_win, o_c, gates, w_mg, w_branch.astype(BF16),
                  w_out.astype(BF16), ln_g, ln_b, B, S)


def kernel(x, p, ffn_w_in, ffn_w_out, ln_g, ln_b, w_in, diff_lam, diff_subln_g, nsa_cmp_pos,
           nsa_cmp_w1, nsa_cmp_w2, swa_sinks, w_branch, w_out, ple_w_in, ple_w_gate):
    B, S, D = x.shape
    T = B * S
    h = x.reshape(T, D)
    ffn_in, ffn_out = ffn_w_in.astype(BF16), ffn_w_out.astype(BF16)
    ple_args = (p.reshape(DEPTH, T, PLE_DIM), ple_w_gate.astype(BF16), ple_w_in.astype(BF16))
    for i in range(DEPTH):
        lambda_init = 0.8 - 0.6 * math.exp(-0.3 * i)
        lg = ln_g[i].reshape(3, 1, D)
        lb = ln_b[i].reshape(3, 1, D)
        h = _ffn(h, ffn_in, ffn_out, (i, 0), lg[0], lb[0])
        h = _token_mixing(h, B, S, w_in[i], diff_lam[i], diff_subln_g[i], nsa_cmp_pos[i],
                          nsa_cmp_w1[i], nsa_cmp_w2[i], swa_sinks[i], w_branch[i], w_out[i],
                          lg[1], lb[1], lambda_init)
        h = _ffn(h, ffn_in, ffn_out, (i, 1), lg[2], lb[2], ple_args=ple_args)
    return h.reshape(B, S, D)
```
